```python
import math
import jax, jax.numpy as jnp
from jax import lax
import numpy as np

D_MODEL = 1024
BATCH = 1
SEQ = 16384
DEPTH = 4

CHUNK = 64
N_MIXERS = 2
N_A = (DEPTH + 1) // 2
N_B = DEPTH // 2
N_HEADS_A = 16
HEAD_DIM_A = D_MODEL // N_HEADS_A
Q_BLOCK = 128
D_RNN = D_MODEL
N_BLOCKS_B = 8
BLOCK_B = D_RNN // N_BLOCKS_B
CONV_B = 4
LRU_C = 8.0
D_FF = 2816
CONV_F = 3
D_PLE = 256
LN_EPS = 1e-5
ALPHA = (2.0 * DEPTH) ** 0.25
BETA = (8.0 * DEPTH) ** -0.25

kernel_name = "fox_rglru_deepnorm_convffn_hybrid"


def layer_norm(x, g, b):
    xf = x.astype(jnp.float32)
    mu = jnp.mean(xf, axis=-1, keepdims=True)
    xc = xf - mu
    var = jnp.mean(xc * xc, axis=-1, keepdims=True)
    y = xc * lax.rsqrt(var + LN_EPS) * g.astype(jnp.float32) + b.astype(jnp.float32)
    return y.astype(x.dtype)


def causal_dwconv(x, w, b):
    k, c = w.shape
    y = lax.conv_general_dilated(
        x, w[:, None, :].astype(x.dtype), window_strides=(1,), padding=[(k - 1, 0)],
        dimension_numbers=("NWC", "WIO", "NWC"), feature_group_count=c)
    return y + b.astype(x.dtype)


def forgetting_attention(x, w_in, b_f, w_out):
    bsz, s, _ = x.shape
    h, dh = N_HEADS_A, HEAD_DIM_A
    proj = x @ w_in
    q, k, v, fg = jnp.split(proj, [D_MODEL, 2 * D_MODEL, 3 * D_MODEL], axis=-1)
    q = q.reshape(bsz, s, h, dh).transpose(0, 2, 1, 3)
    k = k.reshape(bsz, s, h, dh).transpose(0, 2, 1, 3)
    v = v.reshape(bsz, s, h, dh).transpose(0, 2, 1, 3)
    log_f = jax.nn.log_sigmoid((fg + b_f).astype(jnp.float32))
    c = jnp.cumsum(log_f, axis=1).transpose(0, 2, 1)
    n_blk = s // Q_BLOCK
    qb = q.reshape(bsz, h, n_blk, Q_BLOCK, dh).transpose(2, 0, 1, 3, 4)
    cb = c.reshape(bsz, h, n_blk, Q_BLOCK).transpose(2, 0, 1, 3)
    pos_k = jnp.arange(s)
    scale = 1.0 / math.sqrt(dh)

    def one_block(args):
        q_i, c_i, i = args
        logits = jnp.einsum("bhqd,bhkd->bhqk", q_i, k).astype(jnp.float32) * scale
        logits = logits + c_i[..., None] - c[:, :, None, :]
        pos_q = i * Q_BLOCK + jnp.arange(Q_BLOCK)
        mask = pos_k[None, :] <= pos_q[:, None]
        logits = jnp.where(mask, logits, -jnp.inf)
        probs = jax.nn.softmax(logits, axis=-1)
        return jnp.einsum("bhqk,bhkd->bhqd", probs.astype(v.dtype), v)

    o = lax.map(one_block, (qb, cb, jnp.arange(n_blk)))
    o = o.transpose(1, 0, 3, 2, 4).reshape(bsz, s, D_MODEL)
    return o @ w_out


def rglru_block(x, w_in, conv_w, conv_b, w_a, b_a, w_i, b_i, lam, w_out):
    bsz, s, _ = x.shape
    proj = x @ w_in
    xb, gb = jnp.split(proj, [D_RNN], axis=-1)
    xb = causal_dwconv(xb, conv_w, conv_b)
    xh = xb.reshape(bsz, s, N_BLOCKS_B, BLOCK_B)
    r = jax.nn.sigmoid(jnp.einsum("bsnc,ncd->bsnd", xh, w_a) + b_a).reshape(bsz, s, D_RNN)
    ig = jax.nn.sigmoid(jnp.einsum("bsnc,ncd->bsnd", xh, w_i) + b_i).reshape(bsz, s, D_RNN)
    log_a = -LRU_C * r.astype(jnp.float32) * jax.nn.softplus(-lam.astype(jnp.float32))
    a = jnp.exp(log_a)
    mult = jnp.sqrt(-jnp.expm1(2.0 * log_a))
    u = mult * (ig * xb).astype(jnp.float32)

    def combine(left, right):
        a1, b1 = left
        a2, b2 = right
        return a1 * a2, a2 * b1 + b2

    _, hseq = lax.associative_scan(combine, (a, u), axis=1)
    y = hseq.astype(x.dtype) * jax.nn.gelu(gb)
    return y @ w_out


def conv_ffn(x, w_up, conv_w, conv_b, w_down):
    hdn = x @ w_up
    hdn = causal_dwconv(hdn, conv_w, conv_b)
    val, gate = jnp.split(hdn, [D_FF], axis=-1)
    return (jax.nn.gelu(gate) * val) @ w_down


def setup_inputs(seed: int = 0) -> dict:
    key = jax.random.key(seed)
    ks = jax.random.split(key, 32)
    f32 = jnp.float32
    nrm = lambda k, shape, sc: jax.random.normal(k, shape, f32) * sc
    x = nrm(ks[0], (BATCH, SEQ, D_MODEL), 1.0)
    p = nrm(ks[1], (DEPTH, BATCH, SEQ, D_PLE), 1.0)
    col_scale = jnp.concatenate([
        jnp.ones((2 * D_MODEL,), f32),
        jnp.full((D_MODEL,), BETA, f32),
        jnp.full((N_HEADS_A,), 0.1, f32)])
    a_w_in = nrm(ks[2], (N_A, D_MODEL, 3 * D_MODEL + N_HEADS_A), D_MODEL ** -0.5) * col_scale
    a_b_f = jnp.linspace(1.0, 6.0, N_HEADS_A, dtype=f32)[None, :] + nrm(ks[3], (N_A, N_HEADS_A), 0.1)
    a_w_out = nrm(ks[4], (N_A, D_MODEL, D_MODEL), D_MODEL ** -0.5 * BETA)
    b_w_in = nrm(ks[5], (N_B, D_MODEL, 2 * D_RNN), D_MODEL ** -0.5)
    b_conv_w = nrm(ks[6], (N_B, CONV_B, D_RNN), CONV_B ** -0.5)
    b_conv_b = nrm(ks[7], (N_B, D_RNN), 0.02)
    b_w_a = nrm(ks[8], (N_B, N_BLOCKS_B, BLOCK_B, BLOCK_B), BLOCK_B ** -0.5)
    b_b_a = nrm(ks[9], (N_B, N_BLOCKS_B, BLOCK_B), 0.02)
    b_w_i = nrm(ks[10], (N_B, N_BLOCKS_B, BLOCK_B, BLOCK_B), BLOCK_B ** -0.5)
    b_b_i = nrm(ks[11], (N_B, N_BLOCKS_B, BLOCK_B), 0.02)
    a_pow_c = jax.random.uniform(ks[12], (N_B, D_RNN), f32, 0.9, 0.999)
    a_base = a_pow_c ** (1.0 / LRU_C)
    b_lam = jnp.log(a_base) - jnp.log1p(-a_base)
    b_w_out = nrm(ks[13], (N_B, D_RNN, D_MODEL), D_RNN ** -0.5 * BETA)
    f_w_up = nrm(ks[14], (DEPTH, D_MODEL, 2 * D_FF), D_MODEL ** -0.5)
    f_conv_w = nrm(ks[15], (DEPTH, CONV_F, 2 * D_FF), CONV_F ** -0.5)
    f_conv_b = nrm(ks[16], (DEPTH, 2 * D_FF), 0.02)
    f_w_down = nrm(ks[17], (DEPTH, D_FF, D_MODEL), D_FF ** -0.5 * BETA)
    ln1_g = 1.0 + nrm(ks[18], (DEPTH, D_MODEL), 0.02)
    ln1_b = nrm(ks[19], (DEPTH, D_MODEL), 0.02)
    ln2_g = 1.0 + nrm(ks[20], (DEPTH, D_MODEL), 0.02)
    ln2_b = nrm(ks[21], (DEPTH, D_MODEL), 0.02)
    ple_w = nrm(ks[22], (DEPTH, D_PLE, D_MODEL), D_PLE ** -0.5 * BETA)
    ple_gate_w = nrm(ks[23], (DEPTH, D_MODEL, D_MODEL), D_MODEL ** -0.5)
    ple_gate_b = nrm(ks[24], (DEPTH, D_MODEL), 0.02)
    return {"x": x, "p": p,
            "a_w_in": a_w_in, "a_b_f": a_b_f, "a_w_out": a_w_out,
            "b_w_in": b_w_in, "b_conv_w": b_conv_w, "b_conv_b": b_conv_b,
            "b_w_a": b_w_a, "b_b_a": b_b_a, "b_w_i": b_w_i, "b_b_i": b_b_i,
            "b_lam": b_lam, "b_w_out": b_w_out,
            "f_w_up": f_w_up, "f_conv_w": f_conv_w, "f_conv_b": f_conv_b, "f_w_down": f_w_down,
            "ln1_g": ln1_g, "ln1_b": ln1_b, "ln2_g": ln2_g, "ln2_b": ln2_b,
            "ple_w": ple_w, "ple_gate_w": ple_gate_w, "ple_gate_b": ple_gate_b}


def reference(x, p, a_w_in, a_b_f, a_w_out,
              b_w_in, b_conv_w, b_conv_b, b_w_a, b_b_a, b_w_i, b_b_i, b_lam, b_w_out,
              f_w_up, f_conv_w, f_conv_b, f_w_down,
              ln1_g, ln1_b, ln2_g, ln2_b,
              ple_w, ple_gate_w, ple_gate_b):
    for i in range(DEPTH):
        j = i // N_MIXERS
        if i % N_MIXERS == 0:
            m = forgetting_attention(x, a_w_in[j], a_b_f[j], a_w_out[j])
        else:
            m = rglru_block(x, b_w_in[j], b_conv_w[j], b_conv_b[j], b_w_a[j], b_b_a[j],
                            b_w_i[j], b_b_i[j], b_lam[j], b_w_out[j])
        x = layer_norm(ALPHA * x + m, ln1_g[i], ln1_b[i])
        ff = conv_ffn(x, f_w_up[i], f_conv_w[i], f_conv_b[i], f_w_down[i])
        x = layer_norm(ALPHA * x + ff, ln2_g[i], ln2_b[i])
        gate = jax.nn.sigmoid(x @ ple_gate_w[i] + ple_gate_b[i])
        x = x + gate * (p[i] @ ple_w[i])
    return x
```

```python
import functools
import math

import jax
import jax.numpy as jnp
from jax import lax
from jax.experimental import pallas as pl
from jax.experimental.pallas import tpu as pltpu

N_HEADS = 16
LRU_C = 8.0
LN_EPS = 1e-5
DEPTH = 4
ALPHA = (2.0 * DEPTH) ** 0.25
LOG2E = math.log2(math.e)

LANES = 128
SUBLANES = 8
VMEM_LIMIT_BYTES = 56 * 1024 * 1024

TM_PROJ = 512
TQ = 512
TK = 512
TM_OUT = 512
TM_FFN = 256
TN_FFN = 256
TM_LRU = 256

AUG_WIDTH = 6
NEG_BIG = -1e30

BF16 = jnp.bfloat16
F32 = jnp.float32


def _resident(shape):
    n = len(shape)
    return pl.BlockSpec(shape, lambda *_: (0,) * n, pipeline_mode=pl.Buffered(1))


def _dot(a, b):
    return jnp.dot(a, b, preferred_element_type=F32)


def _layer_norm(y, g, b):
    mu = jnp.mean(y, axis=-1, keepdims=True)
    yc = y - mu
    var = jnp.mean(yc * yc, axis=-1, keepdims=True)
    return yc * lax.rsqrt(var + LN_EPS) * g + b


def _split3(x):
    p1 = x.astype(BF16)
    r1 = x - p1.astype(F32)
    p2 = r1.astype(BF16)
    p3 = (r1 - p2.astype(F32)).astype(BF16)
    return p1, p2, p3


def _shift_rows(prev8, cur, k):
    both = jnp.concatenate([prev8, cur], axis=0)
    return pltpu.roll(both, k, 0)[SUBLANES:]


def _attn_proj_kernel(x_ref, wq_ref, wk_ref, wv_ref, wf_ref, bf_ref, tri_ref, pq_ref, pk_ref,
                      q_ref, k_ref, v_ref, carry_ref, *, n_heads, scale):
    tm = x_ref.shape[0]

    @pl.when(pl.program_id(0) == 0)
    def _():
        carry_ref[...] = jnp.zeros_like(carry_ref)

    xb = x_ref[...].astype(BF16)
    lane = lax.broadcasted_iota(jnp.int32, (tm, LANES), 1)

    fg = _dot(xb, wf_ref[...]) + bf_ref[...]
    lf = (jnp.minimum(fg, 0.0) - jnp.log1p(jnp.exp(-jnp.abs(fg)))) * LOG2E
    lf = jnp.where(lane < n_heads, lf, 0.0)

    cs = _dot(tri_ref[...], jnp.concatenate(_split3(lf), axis=1))
    c = cs[:, :LANES] + cs[:, LANES:2 * LANES] + cs[:, 2 * LANES:] + carry_ref[0:1, :]
    carry_ref[...] = jnp.broadcast_to(c[tm - 1:tm, :], carry_ref.shape)

    c1, c2, c3 = _split3(c)
    cz = (c1.astype(F32) + pltpu.roll(c2.astype(F32), n_heads, 1)
          + pltpu.roll(c3.astype(F32), 2 * n_heads, 1))
    cz = jnp.where(lane == 3 * n_heads, 1.0, cz).astype(BF16)
    eq = _dot(cz, pq_ref[...])
    ek = _dot(cz, pk_ref[...])

    qf = _dot(xb, wq_ref[...]) * (scale * LOG2E)
    kf = _dot(xb, wk_ref[...])
    half = LANES // 2
    for pair in range(n_heads // 2):
        cols = slice(pair * LANES, (pair + 1) * LANES)
        ev = slice(2 * pair * LANES, (2 * pair + 1) * LANES)
        od = slice((2 * pair + 1) * LANES, (2 * pair + 2) * LANES)
        q_ref[2 * pair] = jnp.where(lane < half, qf[:, cols], eq[:, ev]).astype(BF16)
        q_ref[2 * pair + 1] = jnp.where(lane >= half, qf[:, cols], eq[:, od]).astype(BF16)
        k_ref[2 * pair] = jnp.where(lane < half, kf[:, cols], ek[:, ev]).astype(BF16)
        k_ref[2 * pair + 1] = jnp.where(lane >= half, kf[:, cols], ek[:, od]).astype(BF16)

    v_ref[...] = _dot(xb, wv_ref[...]).astype(BF16)


def _placement_matrices(n_heads):
    import numpy as np
    pq = np.zeros((LANES, n_heads * LANES), np.float32)
    pk = np.zeros((LANES, n_heads * LANES), np.float32)
    one_lane = 3 * n_heads
    for h in range(n_heads):
        base = h * LANES + (LANES // 2 if h % 2 == 0 else 0)
        for piece in range(3):
            pq[piece * n_heads + h, base + piece] = 1.0
            pq[one_lane, base + 3 + piece] = 1.0
            pk[one_lane, base + piece] = 1.0
            pk[piece * n_heads + h, base + 3 + piece] = -1.0
    return jnp.asarray(pq, BF16), jnp.asarray(pk, BF16)


def _attn_proj(x, w_in, b_f):
    s, d = x.shape
    n_heads = b_f.shape[0]
    dh = d // n_heads
    assert 2 * dh == LANES and 3 * n_heads < LANES and s % TM_PROJ == 0
    wq = w_in[:, :d].astype(BF16)
    wk = w_in[:, d:2 * d].astype(BF16)
    wv = w_in[:, 2 * d:3 * d].astype(BF16)
    wf = jnp.pad(w_in[:, 3 * d:], ((0, 0), (0, LANES - n_heads))).astype(BF16)
    bf = jnp.pad(b_f, (0, LANES - n_heads)).reshape(1, LANES)
    tri = jnp.tril(jnp.ones((TM_PROJ, TM_PROJ), BF16))
    pq, pk = _placement_matrices(n_heads)
    kern = functools.partial(_attn_proj_kernel, n_heads=n_heads, scale=1.0 / math.sqrt(dh))
    return pl.pallas_call(
        kern,
        grid=(s // TM_PROJ,),
        in_specs=[
            pl.BlockSpec((TM_PROJ, d), lambda i: (i, 0)),
            _resident((d, d)), _resident((d, d)), _resident((d, d)),
            _resident((d, LANES)), _resident((1, LANES)),
            _resident((TM_PROJ, TM_PROJ)),
            _resident((LANES, n_heads * LANES)), _resident((LANES, n_heads * LANES)),
        ],
        out_specs=[
            pl.BlockSpec((n_heads, TM_PROJ, LANES), lambda i: (0, i, 0)),
            pl.BlockSpec((n_heads, TM_PROJ, LANES), lambda i: (0, i, 0)),
            pl.BlockSpec((TM_PROJ, d), lambda i: (i, 0)),
        ],
        out_shape=[
            jax.ShapeDtypeStruct((n_heads, s, LANES), BF16),
            jax.ShapeDtypeStruct((n_heads, s, LANES), BF16),
            jax.ShapeDtypeStruct((s, d), BF16),
        ],
        scratch_shapes=[pltpu.VMEM((SUBLANES, LANES), F32)],
        compiler_params=pltpu.CompilerParams(
            dimension_semantics=("arbitrary",), vmem_limit_bytes=VMEM_LIMIT_BYTES),
        name="attn_proj",
    )(x, wq, wk, wv, wf, bf, tri, pq, pk)


def _flash_kernel(q_ref, k_ref, v_ref, o_ref):
    tq = q_ref.shape[1]
    qi = pl.program_id(1)
    lane = lax.broadcasted_iota(jnp.int32, (tq, LANES), 1)
    nt_dims = (((1,), (1,)), ((), ()))

    def head(hh):
        q = q_ref[hh]

        def step(k, v, carry, mask):
            m, l, acc = carry
            s = lax.dot_general(q, k, nt_dims, preferred_element_type=F32)
            if mask is not None:
                s = jnp.where(mask, s, NEG_BIG)
            m_new = jnp.maximum(m, jnp.max(s, axis=1, keepdims=True))
            p = jnp.exp2(s - m_new)
            a = jnp.exp2(m - m_new)
            l = a * l + jnp.sum(p, axis=1, keepdims=True)
            acc = a * acc + _dot(p.astype(BF16), v)
            return m_new, l, acc

        def body(j, carry):
            start = pl.multiple_of(j * TK, TK)
            return step(k_ref[hh, pl.ds(start, TK), :], v_ref[pl.ds(start, TK), :], carry, None)

        init = (jnp.full((tq, 1), NEG_BIG, F32), jnp.zeros((tq, 1), F32),
                jnp.zeros((tq, LANES), F32))
        carry = lax.fori_loop(0, qi * (tq // TK), body, init)
        for jd in range(tq // TK):
            start = pl.multiple_of(qi * tq + jd * TK, TK)
            row = lax.broadcasted_iota(jnp.int32, (tq, TK), 0)
            col = lax.broadcasted_iota(jnp.int32, (tq, TK), 1) + jd * TK
            carry = step(k_ref[hh, pl.ds(start, TK), :], v_ref[pl.ds(start, TK), :], carry,
                         col <= row)
        _, l, acc = carry
        return acc / l

    o0 = head(0)
    o1 = head(1)
    o_ref[...] = jnp.where(lane < LANES // 2, o0, o1).astype(o_ref.dtype)


def _flash(q_aug, k_aug, v):
    n_heads, s, _ = q_aug.shape
    d = v.shape[1]
    assert s % TQ == 0 and TQ % TK == 0
    return pl.pallas_call(
        _flash_kernel,
        grid=(n_heads // 2, s // TQ),
        in_specs=[
            pl.BlockSpec((2, TQ, LANES), lambda hp, qi: (hp, qi, 0)),
            pl.BlockSpec((2, s, LANES), lambda hp, qi: (hp, 0, 0)),
            pl.BlockSpec((s, LANES), lambda hp, qi: (0, hp)),
        ],
        out_specs=pl.BlockSpec((TQ, LANES), lambda hp, qi: (qi, hp)),
        out_shape=jax.ShapeDtypeStruct((s, d), BF16),
        compiler_params=pltpu.CompilerParams(
            dimension_semantics=("arbitrary", "arbitrary"), vmem_limit_bytes=VMEM_LIMIT_BYTES),
        name="flash_attn",
    )(q_aug, k_aug, v)


def _attn_out_kernel(o_ref, w_ref, x_ref, g_ref, b_ref, y_ref):
    m = _dot(o_ref[...], w_ref[...])
    y_ref[...] = _layer_norm(ALPHA * x_ref[...] + m, g_ref[...], b_ref[...])


def _attn_out(o, w_out, x, g, b):
    s, d = x.shape
    assert s % TM_OUT == 0
    return pl.pallas_call(
        _attn_out_kernel,
        grid=(s // TM_OUT,),
        in_specs=[
            pl.BlockSpec((TM_OUT, d), lambda i: (i, 0)),
            _resident((d, d)),
            pl.BlockSpec((TM_OUT, d), lambda i: (i, 0)),
            _resident((1, d)), _resident((1, d)),
        ],
        out_specs=pl.BlockSpec((TM_OUT, d), lambda i: (i, 0)),
        out_shape=jax.ShapeDtypeStruct((s, d), F32),
        compiler_params=pltpu.CompilerParams(
            dimension_semantics=("arbitrary",), vmem_limit_bytes=VMEM_LIMIT_BYTES),
        name="attn_out",
    )(o, w_out.astype(BF16), x, g.reshape(1, d), b.reshape(1, d))


def _rglru_kernel(x_ref, wx_ref, wg_ref, cw_ref, cb_ref, wa_ref, ba_ref, wi_ref, bi_ref,
                  lam_ref, wo_ref, g_ref, b_ref, y_ref, xprev_ref, hprev_ref):
    tm, d = x_ref.shape
    n_blocks = wa_ref.shape[0]
    blk = d // n_blocks
    n_taps = cw_ref.shape[0]

    @pl.when(pl.program_id(0) == 0)
    def _():
        xprev_ref[...] = jnp.zeros_like(xprev_ref)
        hprev_ref[...] = jnp.zeros_like(hprev_ref)

    x = x_ref[...]
    xb = x.astype(BF16)
    xp = _dot(xb, wx_ref[...])
    gb = _dot(xb, wg_ref[...])

    prev8 = xprev_ref[...]
    xc = cb_ref[...] + cw_ref[n_taps - 1:n_taps, :] * xp
    for k in range(1, n_taps):
        xc = xc + cw_ref[n_taps - 1 - k:n_taps - k, :] * _shift_rows(prev8, xp, k)
    xprev_ref[...] = xp[tm - SUBLANES:, :]

    r_parts, i_parts = [], []
    for n in range(n_blocks):
        xh = xc[:, n * blk:(n + 1) * blk].astype(BF16)
        r_parts.append(_dot(xh, wa_ref[n]))
        i_parts.append(_dot(xh, wi_ref[n]))
    r = jax.nn.sigmoid(jnp.concatenate(r_parts, axis=1) + ba_ref[...])
    ig = jax.nn.sigmoid(jnp.concatenate(i_parts, axis=1) + bi_ref[...])

    log_a = (-LRU_C) * r * jax.nn.softplus(-lam_ref[...])
    a = jnp.exp(log_a)
    u = jnp.sqrt(-jnp.tanh(log_a) * (a * a + 1.0)) * (ig * xc)

    row = lax.broadcasted_iota(jnp.int32, (tm, d), 0)
    dist = 1
    while dist < tm:
        if dist < SUBLANES:
            a_sh = jnp.where(row >= dist, pltpu.roll(a, dist, 0), 1.0)
            u_sh = jnp.where(row >= dist, pltpu.roll(u, dist, 0), 0.0)
        else:
            a_sh = jnp.concatenate([jnp.ones((dist, d), F32), a[:tm - dist]], axis=0)
            u_sh = jnp.concatenate([jnp.zeros((dist, d), F32), u[:tm - dist]], axis=0)
        u = u + a * u_sh
        a = a * a_sh
        dist *= 2
    h = u + a * hprev_ref[0:1, :]
    hprev_ref[...] = jnp.broadcast_to(h[tm - 1:tm, :], hprev_ref.shape)

    yv = (h * jax.nn.gelu(gb)).astype(BF16)
    m = _dot(yv, wo_ref[...])
    y_ref[...] = _layer_norm(ALPHA * x + m, g_ref[...], b_ref[...])


def _rglru_layer(x, w_in, conv_w, conv_b, w_a, b_a, w_i, b_i, lam, w_out, g, b):
    s, d = x.shape
    n_blocks, blk, _ = w_a.shape
    assert s % TM_LRU == 0 and conv_w.shape[0] - 1 < SUBLANES
    row = lambda v: v.reshape(1, d)
    return pl.pallas_call(
        _rglru_kernel,
        grid=(s // TM_LRU,),
        in_specs=[
            pl.BlockSpec((TM_LRU, d), lambda i: (i, 0)),
            _resident((d, d)), _resident((d, d)),
            _resident(conv_w.shape), _resident((1, d)),
            _resident(w_a.shape), _resident((1, d)),
            _resident(w_i.shape), _resident((1, d)),
            _resident((1, d)),
            _resident((d, d)), _resident((1, d)), _resident((1, d)),
        ],
        out_specs=pl.BlockSpec((TM_LRU, d), lambda i: (i, 0)),
        out_shape=jax.ShapeDtypeStruct((s, d), F32),
        scratch_shapes=[pltpu.VMEM((SUBLANES, d), F32), pltpu.VMEM((SUBLANES, d), F32)],
        compiler_params=pltpu.CompilerParams(
            dimension_semantics=("arbitrary",), vmem_limit_bytes=VMEM_LIMIT_BYTES),
        name="rglru",
    )(x, w_in[:, :d].astype(BF16), w_in[:, d:].astype(BF16), conv_w, row(conv_b),
      w_a.astype(BF16), row(b_a), w_i.astype(BF16), row(b_i), row(lam),
      w_out.astype(BF16), row(g), row(b))


def _ffn_kernel(x_ref, wv_ref, wg_ref, cwv_ref, cwg_ref, cbv_ref, cbg_ref, wd_ref, g_ref, b_ref,
                p_ref, wp_ref, wpg_ref, bpg_ref, y_ref, vprev_ref, gprev_ref):
    tm, d = x_ref.shape
    n_chunks = wv_ref.shape[0]
    n_taps = cwv_ref.shape[1]

    @pl.when(pl.program_id(0) == 0)
    def _():
        vprev_ref[...] = jnp.zeros_like(vprev_ref)
        gprev_ref[...] = jnp.zeros_like(gprev_ref)

    x = x_ref[...]
    xb = x.astype(BF16)

    def conv(h, prev_ref, cw_ref, cb_ref, j):
        prev8 = prev_ref[j]
        cw = cw_ref[j]
        out = cb_ref[j] + cw[n_taps - 1:n_taps, :] * h
        for k in range(1, n_taps):
            out = out + cw[n_taps - 1 - k:n_taps - k, :] * _shift_rows(prev8, h, k)
        prev_ref[j] = h[tm - SUBLANES:, :]
        return out

    def chunk(j, acc):
        val = conv(_dot(xb, wv_ref[j]), vprev_ref, cwv_ref, cbv_ref, j)
        gate = conv(_dot(xb, wg_ref[j]), gprev_ref, cwg_ref, cbg_ref, j)
        act = (jax.nn.gelu(gate) * val).astype(BF16)
        return acc + _dot(act, wd_ref[j])

    ff = lax.fori_loop(0, n_chunks, chunk, jnp.zeros((tm, d), F32))
    x2 = _layer_norm(ALPHA * x + ff, g_ref[...], b_ref[...])
    gate = jax.nn.sigmoid(_dot(x2.astype(BF16), wpg_ref[...]) + bpg_ref[...])
    y_ref[...] = x2 + gate * _dot(p_ref[...].astype(BF16), wp_ref[...])


def _ffn_layer(x, w_up, conv_w, conv_b, w_down, g, b, p, ple_w, ple_gate_w, ple_gate_b):
    s, d = x.shape
    d_ff = w_down.shape[0]
    d_ple = p.shape[1]
    n_taps = conv_w.shape[0]
    assert s % TM_FFN == 0 and d_ff % TN_FFN == 0 and n_taps - 1 < SUBLANES
    nc = d_ff // TN_FFN
    chunks = lambda w: w.reshape(w.shape[0], nc, TN_FFN).transpose(1, 0, 2)
    row = lambda v: v.reshape(1, d)
    wv, wg = chunks(w_up[:, :d_ff].astype(BF16)), chunks(w_up[:, d_ff:].astype(BF16))
    cwv, cwg = chunks(conv_w[:, :d_ff]), chunks(conv_w[:, d_ff:])
    cbv, cbg = conv_b[:d_ff].reshape(nc, 1, TN_FFN), conv_b[d_ff:].reshape(nc, 1, TN_FFN)
    wd = w_down.astype(BF16).reshape(nc, TN_FFN, d)
    return pl.pallas_call(
        _ffn_kernel,
        grid=(s // TM_FFN,),
        in_specs=[
            pl.BlockSpec((TM_FFN, d), lambda i: (i, 0)),
            _resident(wv.shape), _resident(wg.shape),
            _resident(cwv.shape), _resident(cwg.shape),
            _resident(cbv.shape), _resident(cbg.shape),
            _resident(wd.shape), _resident((1, d)), _resident((1, d)),
            pl.BlockSpec((TM_FFN, d_ple), lambda i: (i, 0)),
            _resident((d_ple, d)), _resident((d, d)), _resident((1, d)),
        ],
        out_specs=pl.BlockSpec((TM_FFN, d), lambda i: (i, 0)),
        out_shape=jax.ShapeDtypeStruct((s, d), F32),
        scratch_shapes=[pltpu.VMEM((nc, SUBLANES, TN_FFN), F32),
                        pltpu.VMEM((nc, SUBLANES, TN_FFN), F32)],
        compiler_params=pltpu.CompilerParams(
            dimension_semantics=("arbitrary",), vmem_limit_bytes=VMEM_LIMIT_BYTES),
        name="conv_ffn",
    )(x, wv, wg, cwv, cwg, cbv, cbg, wd, row(g), row(b), p,
      ple_w.astype(BF16), ple_gate_w.astype(BF16), row(ple_gate_b))


def kernel(x, p, a_w_in, a_b_f, a_w_out, b_w_in, b_conv_w, b_conv_b, b_w_a, b_b_a, b_w_i, b_b_i,
           b_lam, b_w_out, f_w_up, f_conv_w, f_conv_b, f_w_down, ln1_g, ln1_b, ln2_g, ln2_b,
           ple_w, ple_gate_w, ple_gate_b):
    bsz, s, d = x.shape
    depth = p.shape[0]
    outs = []
    for bi in range(bsz):
        xs = x[bi]
        for i in range(depth):
            j = i // 2
            if i % 2 == 0:
                q_aug, k_aug, v = _attn_proj(xs, a_w_in[j], a_b_f[j])
                o = _flash(q_aug, k_aug, v)
                xs = _attn_out(o, a_w_out[j], xs, ln1_g[i], ln1_b[i])
            else:
                xs = _rglru_layer(xs, b_w_in[j], b_conv_w[j], b_conv_b[j], b_w_a[j], b_b_a[j],
                                  b_w_i[j], b_b_i[j], b_lam[j], b_w_out[j], ln1_g[i], ln1_b[i])
            xs = _ffn_layer(xs, f_w_up[i], f_conv_w[i], f_conv_b[i], f_w_down[i],
                            ln2_g[i], ln2_b[i], p[i, bi], ple_w[i], ple_gate_w[i], ple_gate_b[i])
        outs.append(xs)
    return jnp.stack(outs, axis=0)
```

```python
import functools
import math

import jax
import jax.numpy as jnp
from jax import lax
from jax.experimental import pallas as pl
from jax.experimental.pallas import tpu as pltpu

N_HEADS = 16
LRU_C = 8.0
LN_EPS = 1e-5
DEPTH = 4
ALPHA = (2.0 * DEPTH) ** 0.25
LOG2E = math.log2(math.e)

LANES = 128
SUBLANES = 8
VMEM_LIMIT_BYTES = 56 * 1024 * 1024

TQ = 1024
TK = 256
TM_PROJ = 512
TM_OUT = 512
TM_FFN = 256
TN_FFN = 256
TM_LRU = 256

AUG_WIDTH = 6
VT_ROWS = LANES // 2 + 2 * SUBLANES
NEG_BIG = -1e30

BF16 = jnp.bfloat16
F32 = jnp.float32
NT_DIMS = (((1,), (1,)), ((), ()))


def _resident(shape):
    n = len(shape)
    return pl.BlockSpec(shape, lambda *_: (0,) * n, pipeline_mode=pl.Buffered(1))


def _dot(a, b):
    return jnp.dot(a, b, preferred_element_type=F32)


def _layer_norm(y, g, b):
    mu = jnp.mean(y, axis=-1, keepdims=True)
    yc = y - mu
    var = jnp.mean(yc * yc, axis=-1, keepdims=True)
    return yc * lax.rsqrt(var + LN_EPS) * g + b


def _split3(x):
    p1 = x.astype(BF16)
    r1 = x - p1.astype(F32)
    p2 = r1.astype(BF16)
    p3 = (r1 - p2.astype(F32)).astype(BF16)
    return p1, p2, p3


def _shift_rows(prev8, cur, k):
    both = jnp.concatenate([prev8, cur], axis=0)
    return pltpu.roll(both, k, 0)[SUBLANES:]


def _attn_proj_kernel(x_ref, wq_ref, wk_ref, wvt_ref, wf_ref, bf_ref, tri_ref, pq_ref, pk_ref,
                      q_ref, k_ref, vt_ref, carry_ref, *, n_heads, scale):
    tm = x_ref.shape[0]

    @pl.when(pl.program_id(0) == 0)
    def _():
        carry_ref[...] = jnp.zeros_like(carry_ref)

    xb = x_ref[...].astype(BF16)
    lane = lax.broadcasted_iota(jnp.int32, (tm, LANES), 1)

    fg = _dot(xb, wf_ref[...]) + bf_ref[...]
    lf = (jnp.minimum(fg, 0.0) - jnp.log1p(jnp.exp(-jnp.abs(fg)))) * LOG2E
    lf = jnp.where(lane < n_heads, lf, 0.0)

    cs = _dot(tri_ref[...], jnp.concatenate(_split3(lf), axis=1))
    c = cs[:, :LANES] + cs[:, LANES:2 * LANES] + cs[:, 2 * LANES:] + carry_ref[0:1, :]
    carry_ref[...] = jnp.broadcast_to(c[tm - 1:tm, :], carry_ref.shape)

    c1, c2, c3 = _split3(c)
    cz = (c1.astype(F32) + pltpu.roll(c2.astype(F32), n_heads, 1)
          + pltpu.roll(c3.astype(F32), 2 * n_heads, 1))
    cz = jnp.where(lane == 3 * n_heads, 1.0, cz).astype(BF16)
    eq = _dot(cz, pq_ref[...])
    ek = _dot(cz, pk_ref[...])

    qf = _dot(xb, wq_ref[...]) * (scale * LOG2E)
    kf = _dot(xb, wk_ref[...])
    half = LANES // 2
    for pair in range(n_heads // 2):
        cols = slice(pair * LANES, (pair + 1) * LANES)
        ev = slice(2 * pair * LANES, (2 * pair + 1) * LANES)
        od = slice((2 * pair + 1) * LANES, (2 * pair + 2) * LANES)
        q_ref[2 * pair] = jnp.where(lane < half, qf[:, cols], eq[:, ev]).astype(BF16)
        q_ref[2 * pair + 1] = jnp.where(lane >= half, qf[:, cols], eq[:, od]).astype(BF16)
        k_ref[2 * pair] = jnp.where(lane < half, kf[:, cols], ek[:, ev]).astype(BF16)
        k_ref[2 * pair + 1] = jnp.where(lane >= half, kf[:, cols], ek[:, od]).astype(BF16)

    vt = lax.dot_general(wvt_ref[...], xb, NT_DIMS, preferred_element_type=F32).astype(BF16)
    dh = LANES // 2
    tk = vt_ref.shape[3]
    ones_rows = jnp.where(lax.broadcasted_iota(jnp.int32, (VT_ROWS - dh, tk), 0) == 0, 1.0, 0.0)
    for blk in range(tm // tk):
        for h in range(n_heads):
            vt_ref[blk, h, :dh, :] = vt[h * dh:(h + 1) * dh, blk * tk:(blk + 1) * tk]
            vt_ref[blk, h, dh:, :] = ones_rows.astype(BF16)


def _placement_matrices(n_heads):
    import numpy as np
    pq = np.zeros((LANES, n_heads * LANES), np.float32)
    pk = np.zeros((LANES, n_heads * LANES), np.float32)
    one_lane = 3 * n_heads
    for h in range(n_heads):
        base = h * LANES + (LANES // 2 if h % 2 == 0 else 0)
        for piece in range(3):
            pq[piece * n_heads + h, base + piece] = 1.0
            pq[one_lane, base + 3 + piece] = 1.0
            pk[one_lane, base + piece] = 1.0
            pk[piece * n_heads + h, base + 3 + piece] = -1.0
    return jnp.asarray(pq, BF16), jnp.asarray(pk, BF16)


def _attn_proj(x, w_in, b_f):
    s, d = x.shape
    n_heads = b_f.shape[0]
    dh = d // n_heads
    assert 2 * dh == LANES and 3 * n_heads < LANES and s % TM_PROJ == 0
    wq = w_in[:, :d].astype(BF16)
    wk = w_in[:, d:2 * d].astype(BF16)
    wvt = w_in[:, 2 * d:3 * d].T.astype(BF16)
    wf = jnp.pad(w_in[:, 3 * d:], ((0, 0), (0, LANES - n_heads))).astype(BF16)
    bf = jnp.pad(b_f, (0, LANES - n_heads)).reshape(1, LANES)
    tri = jnp.tril(jnp.ones((TM_PROJ, TM_PROJ), BF16))
    pq, pk = _placement_matrices(n_heads)
    kern = functools.partial(_attn_proj_kernel, n_heads=n_heads, scale=1.0 / math.sqrt(dh))
    return pl.pallas_call(
        kern,
        grid=(s // TM_PROJ,),
        in_specs=[
            pl.BlockSpec((TM_PROJ, d), lambda i: (i, 0)),
            _resident((d, d)), _resident((d, d)), _resident((d, d)),
            _resident((d, LANES)), _resident((1, LANES)),
            _resident((TM_PROJ, TM_PROJ)),
            _resident((LANES, n_heads * LANES)), _resident((LANES, n_heads * LANES)),
        ],
        out_specs=[
            pl.BlockSpec((n_heads, TM_PROJ, LANES), lambda i: (0, i, 0)),
            pl.BlockSpec((n_heads, TM_PROJ, LANES), lambda i: (0, i, 0)),
            pl.BlockSpec((TM_PROJ // TK, n_heads, VT_ROWS, TK), lambda i: (i, 0, 0, 0)),
        ],
        out_shape=[
            jax.ShapeDtypeStruct((n_heads, s, LANES), BF16),
            jax.ShapeDtypeStruct((n_heads, s, LANES), BF16),
            jax.ShapeDtypeStruct((s // TK, n_heads, VT_ROWS, TK), BF16),
        ],
        scratch_shapes=[pltpu.VMEM((SUBLANES, LANES), F32)],
        compiler_params=pltpu.CompilerParams(
            dimension_semantics=("arbitrary",), vmem_limit_bytes=VMEM_LIMIT_BYTES),
        name="attn_proj",
    )(x, wq, wk, wvt, wf, bf, tri, pq, pk)


def _flash_kernel(q_ref, k_ref, vt_ref, o_ref, s_ref):
    tq = q_ref.shape[1]
    tk = vt_ref.shape[3]
    dh = LANES // 2
    qi = pl.program_id(1)
    qs = (q_ref[0], q_ref[1])

    def scores(slot, j):
        for hh in range(2):
            k = k_ref[hh, pl.ds(pl.multiple_of(j * tk, tk), tk), :]
            s_ref[slot, hh] = lax.dot_general(k, qs[hh], NT_DIMS, preferred_element_type=F32)

    def update(slot, j, stats, mask=None):
        out = []
        for hh in range(2):
            m, acc = stats[hh]
            s = s_ref[slot, hh]
            if mask is not None:
                s = jnp.where(mask, s, NEG_BIG)
            m_new = jnp.maximum(m, jnp.max(s, axis=0, keepdims=True))
            p = jnp.exp2(s - m_new).astype(BF16)
            acc = jnp.exp2(m - m_new) * acc + _dot(vt_ref[j, hh], p)
            out.append((m_new, acc))
        return tuple(out)

    n_sub = tq // tk

    def body(jj, stats):
        for b in range(n_sub):
            scores((b + 1) % 2, n_sub * jj + b + 1)
            stats = update(b % 2, n_sub * jj + b, stats)
        return stats

    init = (jnp.full((1, tq), NEG_BIG, F32), jnp.zeros((VT_ROWS, tq), F32))
    scores(0, 0)
    stats = lax.fori_loop(0, qi, body, (init, init))
    key = lax.broadcasted_iota(jnp.int32, (tk, tq), 0)
    qry = lax.broadcasted_iota(jnp.int32, (tk, tq), 1)
    for b in range(n_sub):
        if b + 1 < n_sub:
            scores((b + 1) % 2, n_sub * qi + b + 1)
        stats = update(b % 2, n_sub * qi + b, stats, key + b * tk <= qry)
    outs = [acc[:dh] / acc[dh:dh + 1] for _, acc in stats]
    o_ref[...] = jnp.concatenate(outs, axis=0).T.astype(o_ref.dtype)


def _flash(q_aug, k_aug, vt):
    n_heads, s, _ = q_aug.shape
    n_kv, _, vt_rows, tk = vt.shape
    d = n_heads * (LANES // 2)
    assert TQ % (2 * tk) == 0 and s % TQ == 0
    return pl.pallas_call(
        _flash_kernel,
        grid=(n_heads // 2, s // TQ),
        in_specs=[
            pl.BlockSpec((2, TQ, LANES), lambda hp, qi: (hp, qi, 0)),
            pl.BlockSpec((2, s, LANES), lambda hp, qi: (hp, 0, 0)),
            pl.BlockSpec((n_kv, 2, vt_rows, tk), lambda hp, qi: (0, hp, 0, 0)),
        ],
        out_specs=pl.BlockSpec((TQ, LANES), lambda hp, qi: (qi, hp)),
        out_shape=jax.ShapeDtypeStruct((s, d), BF16),
        scratch_shapes=[pltpu.VMEM((2, 2, tk, TQ), F32)],
        compiler_params=pltpu.CompilerParams(
            dimension_semantics=("arbitrary", "arbitrary"), vmem_limit_bytes=VMEM_LIMIT_BYTES),
        name="flash_attn",
    )(q_aug, k_aug, vt)


def _attn_out_kernel(o_ref, w_ref, x_ref, g_ref, b_ref, y_ref):
    m = _dot(o_ref[...], w_ref[...])
    y_ref[...] = _layer_norm(ALPHA * x_ref[...] + m, g_ref[...], b_ref[...])


def _attn_out(o, w_out, x, g, b):
    s, d = x.shape
    assert s % TM_OUT == 0
    return pl.pallas_call(
        _attn_out_kernel,
        grid=(s // TM_OUT,),
        in_specs=[
            pl.BlockSpec((TM_OUT, d), lambda i: (i, 0)),
            _resident((d, d)),
            pl.BlockSpec((TM_OUT, d), lambda i: (i, 0)),
            _resident((1, d)), _resident((1, d)),
        ],
        out_specs=pl.BlockSpec((TM_OUT, d), lambda i: (i, 0)),
        out_shape=jax.ShapeDtypeStruct((s, d), F32),
        compiler_params=pltpu.CompilerParams(
            dimension_semantics=("arbitrary",), vmem_limit_bytes=VMEM_LIMIT_BYTES),
        name="attn_out",
    )(o, w_out.astype(BF16), x, g.reshape(1, d), b.reshape(1, d))


def _rglru_kernel(x_ref, wx_ref, wg_ref, cw_ref, cb_ref, wa_ref, ba_ref, wi_ref, bi_ref,
                  lam_ref, wo_ref, g_ref, b_ref, y_ref, xprev_ref, hprev_ref):
    tm, d = x_ref.shape
    n_blocks = wa_ref.shape[0]
    blk = d // n_blocks
    n_taps = cw_ref.shape[0]

    @pl.when(pl.program_id(0) == 0)
    def _():
        xprev_ref[...] = jnp.zeros_like(xprev_ref)
        hprev_ref[...] = jnp.zeros_like(hprev_ref)

    x = x_ref[...]
    xb = x.astype(BF16)
    xp = _dot(xb, wx_ref[...])
    gb = _dot(xb, wg_ref[...])

    prev8 = xprev_ref[...]
    xc = cb_ref[...] + cw_ref[n_taps - 1:n_taps, :] * xp
    for k in range(1, n_taps):
        xc = xc + cw_ref[n_taps - 1 - k:n_taps - k, :] * _shift_rows(prev8, xp, k)
    xprev_ref[...] = xp[tm - SUBLANES:, :]

    r_parts, i_parts = [], []
    for n in range(n_blocks):
        xh = xc[:, n * blk:(n + 1) * blk].astype(BF16)
        r_parts.append(_dot(xh, wa_ref[n]))
        i_parts.append(_dot(xh, wi_ref[n]))
    r = jax.nn.sigmoid(jnp.concatenate(r_parts, axis=1) + ba_ref[...])
    ig = jax.nn.sigmoid(jnp.concatenate(i_parts, axis=1) + bi_ref[...])

    log_a = (-LRU_C) * r * jax.nn.softplus(-lam_ref[...])
    a = jnp.exp(log_a)
    u = jnp.sqrt(-jnp.tanh(log_a) * (a * a + 1.0)) * (ig * xc)

    row = lax.broadcasted_iota(jnp.int32, (tm, d), 0)
    dist = 1
    while dist < tm:
        if dist < SUBLANES:
            a_sh = jnp.where(row >= dist, pltpu.roll(a, dist, 0), 1.0)
            u_sh = jnp.where(row >= dist, pltpu.roll(u, dist, 0), 0.0)
        else:
            a_sh = jnp.concatenate([jnp.ones((dist, d), F32), a[:tm - dist]], axis=0)
            u_sh = jnp.concatenate([jnp.zeros((dist, d), F32), u[:tm - dist]], axis=0)
        u = u + a * u_sh
        a = a * a_sh
        dist *= 2
    h = u + a * hprev_ref[0:1, :]
    hprev_ref[...] = jnp.broadcast_to(h[tm - 1:tm, :], hprev_ref.shape)

    yv = (h * jax.nn.gelu(gb)).astype(BF16)
    m = _dot(yv, wo_ref[...])
    y_ref[...] = _layer_norm(ALPHA * x + m, g_ref[...], b_ref[...])


def _rglru_layer(x, w_in, conv_w, conv_b, w_a, b_a, w_i, b_i, lam, w_out, g, b):
    s, d = x.shape
    n_blocks, blk, _ = w_a.shape
    assert s % TM_LRU == 0 and conv_w.shape[0] - 1 < SUBLANES
    row = lambda v: v.reshape(1, d)
    return pl.pallas_call(
        _rglru_kernel,
        grid=(s // TM_LRU,),
        in_specs=[
            pl.BlockSpec((TM_LRU, d), lambda i: (i, 0)),
            _resident((d, d)), _resident((d, d)),
            _resident(conv_w.shape), _resident((1, d)),
            _resident(w_a.shape), _resident((1, d)),
            _resident(w_i.shape), _resident((1, d)),
            _resident((1, d)),
            _resident((d, d)), _resident((1, d)), _resident((1, d)),
        ],
        out_specs=pl.BlockSpec((TM_LRU, d), lambda i: (i, 0)),
        out_shape=jax.ShapeDtypeStruct((s, d), F32),
        scratch_shapes=[pltpu.VMEM((SUBLANES, d), F32), pltpu.VMEM((SUBLANES, d), F32)],
        compiler_params=pltpu.CompilerParams(
            dimension_semantics=("arbitrary",), vmem_limit_bytes=VMEM_LIMIT_BYTES),
        name="rglru",
    )(x, w_in[:, :d].astype(BF16), w_in[:, d:].astype(BF16), conv_w, row(conv_b),
      w_a.astype(BF16), row(b_a), w_i.astype(BF16), row(b_i), row(lam),
      w_out.astype(BF16), row(g), row(b))


def _ffn_kernel(x_ref, wv_ref, wg_ref, cwv_ref, cwg_ref, cbv_ref, cbg_ref, wd_ref, g_ref, b_ref,
                p_ref, wp_ref, wpg_ref, bpg_ref, y_ref, vprev_ref, gprev_ref):
    tm, d = x_ref.shape
    n_chunks = wv_ref.shape[0]
    n_taps = cwv_ref.shape[1]

    @pl.when(pl.program_id(0) == 0)
    def _():
        vprev_ref[...] = jnp.zeros_like(vprev_ref)
        gprev_ref[...] = jnp.zeros_like(gprev_ref)

    x = x_ref[...]
    xb = x.astype(BF16)

    def conv(h, prev_ref, cw_ref, cb_ref, j):
        prev8 = prev_ref[j]
        cw = cw_ref[j]
        out = cb_ref[j] + cw[n_taps - 1:n_taps, :] * h
        for k in range(1, n_taps):
            out = out + cw[n_taps - 1 - k:n_taps - k, :] * _shift_rows(prev8, h, k)
        prev_ref[j] = h[tm - SUBLANES:, :]
        return out

    def chunk(j, acc):
        val = conv(_dot(xb, wv_ref[j]), vprev_ref, cwv_ref, cbv_ref, j)
        gate = conv(_dot(xb, wg_ref[j]), gprev_ref, cwg_ref, cbg_ref, j)
        act = (jax.nn.gelu(gate) * val).astype(BF16)
        return acc + _dot(act, wd_ref[j])

    ff = lax.fori_loop(0, n_chunks, chunk, jnp.zeros((tm, d), F32))
    x2 = _layer_norm(ALPHA * x + ff, g_ref[...], b_ref[...])
    gate = jax.nn.sigmoid(_dot(x2.astype(BF16), wpg_ref[...]) + bpg_ref[...])
    y_ref[...] = x2 + gate * _dot(p_ref[...].astype(BF16), wp_ref[...])


def _ffn_layer(x, w_up, conv_w, conv_b, w_down, g, b, p, ple_w, ple_gate_w, ple_gate_b):
    s, d = x.shape
    d_ff = w_down.shape[0]
    d_ple = p.shape[1]
    n_taps = conv_w.shape[0]
    assert s % TM_FFN == 0 and d_ff % TN_FFN == 0 and n_taps - 1 < SUBLANES
    nc = d_ff // TN_FFN
    chunks = lambda w: w.reshape(w.shape[0], nc, TN_FFN).transpose(1, 0, 2)
    row = lambda v: v.reshape(1, d)
    wv, wg = chunks(w_up[:, :d_ff].astype(BF16)), chunks(w_up[:, d_ff:].astype(BF16))
    cwv, cwg = chunks(conv_w[:, :d_ff]), chunks(conv_w[:, d_ff:])
    cbv, cbg = conv_b[:d_ff].reshape(nc, 1, TN_FFN), conv_b[d_ff:].reshape(nc, 1, TN_FFN)
    wd = w_down.astype(BF16).reshape(nc, TN_FFN, d)
    return pl.pallas_call(
        _ffn_kernel,
        grid=(s // TM_FFN,),
        in_specs=[
            pl.BlockSpec((TM_FFN, d), lambda i: (i, 0)),
            _resident(wv.shape), _resident(wg.shape),
            _resident(cwv.shape), _resident(cwg.shape),
            _resident(cbv.shape), _resident(cbg.shape),
            _resident(wd.shape), _resident((1, d)), _resident((1, d)),
            pl.BlockSpec((TM_FFN, d_ple), lambda i: (i, 0)),
            _resident((d_ple, d)), _resident((d, d)), _resident((1, d)),
        ],
        out_specs=pl.BlockSpec((TM_FFN, d), lambda i: (i, 0)),
        out_shape=jax.ShapeDtypeStruct((s, d), F32),
        scratch_shapes=[pltpu.VMEM((nc, SUBLANES, TN_FFN), F32),
                        pltpu.VMEM((nc, SUBLANES, TN_FFN), F32)],
        compiler_params=pltpu.CompilerParams(
            dimension_semantics=("arbitrary",), vmem_limit_bytes=VMEM_LIMIT_BYTES),
        name="conv_ffn",
    )(x, wv, wg, cwv, cwg, cbv, cbg, wd, row(g), row(b), p,
      ple_w.astype(BF16), ple_gate_w.astype(BF16), row(ple_gate_b))


def kernel(x, p, a_w_in, a_b_f, a_w_out, b_w_in, b_conv_w, b_conv_b, b_w_a, b_b_a, b_w_i, b_b_i,
           b_lam, b_w_out, f_w_up, f_conv_w, f_conv_b, f_w_down, ln1_g, ln1_b, ln2_g, ln2_b,
           ple_w, ple_gate_w, ple_gate_b):
    bsz, s, d = x.shape
    depth = p.shape[0]
    outs = []
    for bi in range(bsz):
        xs = x[bi]
        for i in range(depth):
            j = i // 2
            if i % 2 == 0:
                q_aug, k_aug, v = _attn_proj(xs, a_w_in[j], a_b_f[j])
                o = _flash(q_aug, k_aug, v)
                xs = _attn_out(o, a_w_out[j], xs, ln1_g[i], ln1_b[i])
            else:
                xs = _rglru_layer(xs, b_w_in[j], b_conv_w[j], b_conv_b[j], b_w_a[j], b_b_a[j],
                                  b_w_i[j], b_b_i[j], b_lam[j], b_w_out[j], ln1_g[i], ln1_b[i])
            xs = _ffn_layer(xs, f_w_up[i], f_conv_w[i], f_conv_b[i], f_w_down[i],
                            ln2_g[i], ln2_b[i], p[i, bi], ple_w[i], ple_gate_w[i], ple_gate_b[i])
        outs.append(xs)
    return jnp.stack(outs, axis=0)
```

```python
import functools
import math

import jax
import jax.numpy as jnp
from jax import lax
from jax.experimental import pallas as pl
from jax.experimental.pallas import tpu as pltpu

N_HEADS = 16
LRU_C = 8.0
LN_EPS = 1e-5
DEPTH = 4
ALPHA = (2.0 * DEPTH) ** 0.25
LOG2E = math.log2(math.e)

LANES = 128
SUBLANES = 8
VMEM_LIMIT_BYTES = 56 * 1024 * 1024

TQ = 1024
TK = 256
TM_PROJ = 512
TM_OUT = 512
TM_FFN = 512
TN_FFN = 256
TM_LRU = 256

AUG_WIDTH = 6
VT_ROWS = LANES // 2 + 2 * SUBLANES
NEG_BIG = -1e30

BF16 = jnp.bfloat16
F32 = jnp.float32
NT_DIMS = (((1,), (1,)), ((), ()))


def _resident(shape):
    n = len(shape)
    return pl.BlockSpec(shape, lambda *_: (0,) * n, pipeline_mode=pl.Buffered(1))


def _dot(a, b):
    return jnp.dot(a, b, preferred_element_type=F32)


def _layer_norm(y, g, b):
    mu = jnp.mean(y, axis=-1, keepdims=True)
    yc = y - mu
    var = jnp.mean(yc * yc, axis=-1, keepdims=True)
    return yc * lax.rsqrt(var + LN_EPS) * g + b


def _split3(x):
    p1 = x.astype(BF16)
    r1 = x - p1.astype(F32)
    p2 = r1.astype(BF16)
    p3 = (r1 - p2.astype(F32)).astype(BF16)
    return p1, p2, p3


def _shift_rows(prev8, cur, k):
    both = jnp.concatenate([prev8, cur], axis=0)
    return pltpu.roll(both, k, 0)[SUBLANES:]


def _attn_proj_kernel(x_ref, wq_ref, wk_ref, wvt_ref, wf_ref, bf_ref, tri_ref, pq_ref, pk_ref,
                      q_ref, k_ref, vt_ref, carry_ref, *, n_heads, scale):
    tm = x_ref.shape[0]

    @pl.when(pl.program_id(0) == 0)
    def _():
        carry_ref[...] = jnp.zeros_like(carry_ref)

    xb = x_ref[...].astype(BF16)
    lane = lax.broadcasted_iota(jnp.int32, (tm, LANES), 1)

    fg = _dot(xb, wf_ref[...]) + bf_ref[...]
    lf = (jnp.minimum(fg, 0.0) - jnp.log1p(jnp.exp(-jnp.abs(fg)))) * LOG2E
    lf = jnp.where(lane < n_heads, lf, 0.0)

    cs = _dot(tri_ref[...], jnp.concatenate(_split3(lf), axis=1))
    c = cs[:, :LANES] + cs[:, LANES:2 * LANES] + cs[:, 2 * LANES:] + carry_ref[0:1, :]
    carry_ref[...] = jnp.broadcast_to(c[tm - 1:tm, :], carry_ref.shape)

    c1, c2, c3 = _split3(c)
    cz = (c1.astype(F32) + pltpu.roll(c2.astype(F32), n_heads, 1)
          + pltpu.roll(c3.astype(F32), 2 * n_heads, 1))
    cz = jnp.where(lane == 3 * n_heads, 1.0, cz).astype(BF16)
    eq = _dot(cz, pq_ref[...])
    ek = _dot(cz, pk_ref[...])

    qf = _dot(xb, wq_ref[...]) * (scale * LOG2E)
    kf = _dot(xb, wk_ref[...])
    half = LANES // 2
    for pair in range(n_heads // 2):
        cols = slice(pair * LANES, (pair + 1) * LANES)
        ev = slice(2 * pair * LANES, (2 * pair + 1) * LANES)
        od = slice((2 * pair + 1) * LANES, (2 * pair + 2) * LANES)
        q_ref[2 * pair] = jnp.where(lane < half, qf[:, cols], eq[:, ev]).astype(BF16)
        q_ref[2 * pair + 1] = jnp.where(lane >= half, qf[:, cols], eq[:, od]).astype(BF16)
        k_ref[2 * pair] = jnp.where(lane < half, kf[:, cols], ek[:, ev]).astype(BF16)
        k_ref[2 * pair + 1] = jnp.where(lane >= half, kf[:, cols], ek[:, od]).astype(BF16)

    vt = lax.dot_general(wvt_ref[...], xb, NT_DIMS, preferred_element_type=F32).astype(BF16)
    dh = LANES // 2
    tk = vt_ref.shape[3]
    ones_rows = jnp.where(lax.broadcasted_iota(jnp.int32, (VT_ROWS - dh, tk), 0) == 0, 1.0, 0.0)
    for blk in range(tm // tk):
        for h in range(n_heads):
            vt_ref[blk, h, :dh, :] = vt[h * dh:(h + 1) * dh, blk * tk:(blk + 1) * tk]
            vt_ref[blk, h, dh:, :] = ones_rows.astype(BF16)


def _placement_matrices(n_heads):
    import numpy as np
    pq = np.zeros((LANES, n_heads * LANES), np.float32)
    pk = np.zeros((LANES, n_heads * LANES), np.float32)
    one_lane = 3 * n_heads
    for h in range(n_heads):
        base = h * LANES + (LANES // 2 if h % 2 == 0 else 0)
        for piece in range(3):
            pq[piece * n_heads + h, base + piece] = 1.0
            pq[one_lane, base + 3 + piece] = 1.0
            pk[one_lane, base + piece] = 1.0
            pk[piece * n_heads + h, base + 3 + piece] = -1.0
    return jnp.asarray(pq, BF16), jnp.asarray(pk, BF16)


def _attn_proj(x, w_in, b_f):
    s, d = x.shape
    n_heads = b_f.shape[0]
    dh = d // n_heads
    assert 2 * dh == LANES and 3 * n_heads < LANES and s % TM_PROJ == 0
    wq = w_in[:, :d].astype(BF16)
    wk = w_in[:, d:2 * d].astype(BF16)
    wvt = w_in[:, 2 * d:3 * d].T.astype(BF16)
    wf = jnp.pad(w_in[:, 3 * d:], ((0, 0), (0, LANES - n_heads))).astype(BF16)
    bf = jnp.pad(b_f, (0, LANES - n_heads)).reshape(1, LANES)
    tri = jnp.tril(jnp.ones((TM_PROJ, TM_PROJ), BF16))
    pq, pk = _placement_matrices(n_heads)
    kern = functools.partial(_attn_proj_kernel, n_heads=n_heads, scale=1.0 / math.sqrt(dh))
    return pl.pallas_call(
        kern,
        grid=(s // TM_PROJ,),
        in_specs=[
            pl.BlockSpec((TM_PROJ, d), lambda i: (i, 0)),
            _resident((d, d)), _resident((d, d)), _resident((d, d)),
            _resident((d, LANES)), _resident((1, LANES)),
            _resident((TM_PROJ, TM_PROJ)),
            _resident((LANES, n_heads * LANES)), _resident((LANES, n_heads * LANES)),
        ],
        out_specs=[
            pl.BlockSpec((n_heads, TM_PROJ, LANES), lambda i: (0, i, 0)),
            pl.BlockSpec((n_heads, TM_PROJ, LANES), lambda i: (0, i, 0)),
            pl.BlockSpec((TM_PROJ // TK, n_heads, VT_ROWS, TK), lambda i: (i, 0, 0, 0)),
        ],
        out_shape=[
            jax.ShapeDtypeStruct((n_heads, s, LANES), BF16),
            jax.ShapeDtypeStruct((n_heads, s, LANES), BF16),
            jax.ShapeDtypeStruct((s // TK, n_heads, VT_ROWS, TK), BF16),
        ],
        scratch_shapes=[pltpu.VMEM((SUBLANES, LANES), F32)],
        compiler_params=pltpu.CompilerParams(
            dimension_semantics=("arbitrary",), vmem_limit_bytes=VMEM_LIMIT_BYTES),
        name="attn_proj",
    )(x, wq, wk, wvt, wf, bf, tri, pq, pk)


def _flash_kernel(q_ref, k_ref, vt_ref, o_ref, s_ref):
    tq = q_ref.shape[1]
    tk = vt_ref.shape[3]
    dh = LANES // 2
    qi = pl.program_id(1)
    qs = (q_ref[0], q_ref[1])

    def scores(slot, j):
        for hh in range(2):
            k = k_ref[hh, pl.ds(pl.multiple_of(j * tk, tk), tk), :]
            s_ref[slot, hh] = lax.dot_general(k, qs[hh], NT_DIMS, preferred_element_type=F32)

    def update(slot, j, stats, mask=None):
        out = []
        for hh in range(2):
            m, acc = stats[hh]
            s = s_ref[slot, hh]
            if mask is not None:
                s = jnp.where(mask, s, NEG_BIG)
            m_new = jnp.maximum(m, jnp.max(s, axis=0, keepdims=True))
            p = jnp.exp2(s - m_new).astype(BF16)
            acc = jnp.exp2(m - m_new) * acc + _dot(vt_ref[j, hh], p)
            out.append((m_new, acc))
        return tuple(out)

    n_sub = tq // tk

    def body(jj, stats):
        for b in range(n_sub):
            scores((b + 1) % 2, n_sub * jj + b + 1)
            stats = update(b % 2, n_sub * jj + b, stats)
        return stats

    init = (jnp.full((1, tq), NEG_BIG, F32), jnp.zeros((VT_ROWS, tq), F32))
    scores(0, 0)
    stats = lax.fori_loop(0, qi, body, (init, init))
    key = lax.broadcasted_iota(jnp.int32, (tk, tq), 0)
    qry = lax.broadcasted_iota(jnp.int32, (tk, tq), 1)
    for b in range(n_sub):
        if b + 1 < n_sub:
            scores((b + 1) % 2, n_sub * qi + b + 1)
        stats = update(b % 2, n_sub * qi + b, stats, key + b * tk <= qry)
    outs = [acc[:dh] / acc[dh:dh + 1] for _, acc in stats]
    o_ref[...] = jnp.concatenate(outs, axis=0).T.astype(o_ref.dtype)


def _flash(q_aug, k_aug, vt):
    n_heads, s, _ = q_aug.shape
    n_kv, _, vt_rows, tk = vt.shape
    d = n_heads * (LANES // 2)
    assert TQ % (2 * tk) == 0 and s % TQ == 0
    return pl.pallas_call(
        _flash_kernel,
        grid=(n_heads // 2, s // TQ),
        in_specs=[
            pl.BlockSpec((2, TQ, LANES), lambda hp, qi: (hp, qi, 0)),
            pl.BlockSpec((2, s, LANES), lambda hp, qi: (hp, 0, 0)),
            pl.BlockSpec((n_kv, 2, vt_rows, tk), lambda hp, qi: (0, hp, 0, 0)),
        ],
        out_specs=pl.BlockSpec((TQ, LANES), lambda hp, qi: (qi, hp)),
        out_shape=jax.ShapeDtypeStruct((s, d), BF16),
        scratch_shapes=[pltpu.VMEM((2, 2, tk, TQ), F32)],
        compiler_params=pltpu.CompilerParams(
            dimension_semantics=("arbitrary", "arbitrary"), vmem_limit_bytes=VMEM_LIMIT_BYTES),
        name="flash_attn",
    )(q_aug, k_aug, vt)


def _attn_out_kernel(o_ref, w_ref, x_ref, g_ref, b_ref, y_ref):
    m = _dot(o_ref[...], w_ref[...])
    y_ref[...] = _layer_norm(ALPHA * x_ref[...] + m, g_ref[...], b_ref[...])


def _attn_out(o, w_out, x, g, b):
    s, d = x.shape
    assert s % TM_OUT == 0
    return pl.pallas_call(
        _attn_out_kernel,
        grid=(s // TM_OUT,),
        in_specs=[
            pl.BlockSpec((TM_OUT, d), lambda i: (i, 0)),
            _resident((d, d)),
            pl.BlockSpec((TM_OUT, d), lambda i: (i, 0)),
            _resident((1, d)), _resident((1, d)),
        ],
        out_specs=pl.BlockSpec((TM_OUT, d), lambda i: (i, 0)),
        out_shape=jax.ShapeDtypeStruct((s, d), F32),
        compiler_params=pltpu.CompilerParams(
            dimension_semantics=("arbitrary",), vmem_limit_bytes=VMEM_LIMIT_BYTES),
        name="attn_out",
    )(o, w_out.astype(BF16), x, g.reshape(1, d), b.reshape(1, d))


def _rglru_kernel(x_ref, wx_ref, wg_ref, cw_ref, cb_ref, wa_ref, ba_ref, wi_ref, bi_ref,
                  lam_ref, wo_ref, g_ref, b_ref, y_ref, xprev_ref, hprev_ref):
    tm, d = x_ref.shape
    n_blocks = wa_ref.shape[0]
    blk = d // n_blocks
    n_taps = cw_ref.shape[0]

    @pl.when(pl.program_id(0) == 0)
    def _():
        xprev_ref[...] = jnp.zeros_like(xprev_ref)
        hprev_ref[...] = jnp.zeros_like(hprev_ref)

    x = x_ref[...]
    xb = x.astype(BF16)
    xp = _dot(xb, wx_ref[...])
    gb = _dot(xb, wg_ref[...])

    prev8 = xprev_ref[...]
    xc = cb_ref[...] + cw_ref[n_taps - 1:n_taps, :] * xp
    for k in range(1, n_taps):
        xc = xc + cw_ref[n_taps - 1 - k:n_taps - k, :] * _shift_rows(prev8, xp, k)
    xprev_ref[...] = xp[tm - SUBLANES:, :]

    r_parts, i_parts = [], []
    for n in range(n_blocks):
        xh = xc[:, n * blk:(n + 1) * blk].astype(BF16)
        r_parts.append(_dot(xh, wa_ref[n]))
        i_parts.append(_dot(xh, wi_ref[n]))
    r = jax.nn.sigmoid(jnp.concatenate(r_parts, axis=1) + ba_ref[...])
    ig = jax.nn.sigmoid(jnp.concatenate(i_parts, axis=1) + bi_ref[...])

    log_a = (-LRU_C) * r * jax.nn.softplus(-lam_ref[...])
    a = jnp.exp(log_a)
    u = jnp.sqrt(-jnp.tanh(log_a) * (a * a + 1.0)) * (ig * xc)

    row = lax.broadcasted_iota(jnp.int32, (tm, d), 0)
    dist = 1
    while dist < tm:
        if dist < SUBLANES:
            a_sh = jnp.where(row >= dist, pltpu.roll(a, dist, 0), 1.0)
            u_sh = jnp.where(row >= dist, pltpu.roll(u, dist, 0), 0.0)
        else:
            a_sh = jnp.concatenate([jnp.ones((dist, d), F32), a[:tm - dist]], axis=0)
            u_sh = jnp.concatenate([jnp.zeros((dist, d), F32), u[:tm - dist]], axis=0)
        u = u + a * u_sh
        a = a * a_sh
        dist *= 2
    h = u + a * hprev_ref[0:1, :]
    hprev_ref[...] = jnp.broadcast_to(h[tm - 1:tm, :], hprev_ref.shape)

    yv = (h * jax.nn.gelu(gb)).astype(BF16)
    m = _dot(yv, wo_ref[...])
    y_ref[...] = _layer_norm(ALPHA * x + m, g_ref[...], b_ref[...])


def _rglru_layer(x, w_in, conv_w, conv_b, w_a, b_a, w_i, b_i, lam, w_out, g, b):
    s, d = x.shape
    n_blocks, blk, _ = w_a.shape
    assert s % TM_LRU == 0 and conv_w.shape[0] - 1 < SUBLANES
    row = lambda v: v.reshape(1, d)
    return pl.pallas_call(
        _rglru_kernel,
        grid=(s // TM_LRU,),
        in_specs=[
            pl.BlockSpec((TM_LRU, d), lambda i: (i, 0)),
            _resident((d, d)), _resident((d, d)),
            _resident(conv_w.shape), _resident((1, d)),
            _resident(w_a.shape), _resident((1, d)),
            _resident(w_i.shape), _resident((1, d)),
            _resident((1, d)),
            _resident((d, d)), _resident((1, d)), _resident((1, d)),
        ],
        out_specs=pl.BlockSpec((TM_LRU, d), lambda i: (i, 0)),
        out_shape=jax.ShapeDtypeStruct((s, d), F32),
        scratch_shapes=[pltpu.VMEM((SUBLANES, d), F32), pltpu.VMEM((SUBLANES, d), F32)],
        compiler_params=pltpu.CompilerParams(
            dimension_semantics=("arbitrary",), vmem_limit_bytes=VMEM_LIMIT_BYTES),
        name="rglru",
    )(x, w_in[:, :d].astype(BF16), w_in[:, d:].astype(BF16), conv_w, row(conv_b),
      w_a.astype(BF16), row(b_a), w_i.astype(BF16), row(b_i), row(lam),
      w_out.astype(BF16), row(g), row(b))


def _ffn_kernel(x_ref, wv_ref, wg_ref, cw_ref, cb_ref, wd_ref, g_ref, b_ref,
                p_ref, wp_ref, wpg_ref, bpg_ref, y_ref, prev_ref, h_ref, act_ref):
    tm, d = x_ref.shape
    d_ff = wd_ref.shape[0]
    tn = h_ref.shape[3]
    n_chunks = d_ff // tn
    n_taps = cw_ref.shape[0]

    @pl.when(pl.program_id(0) == 0)
    def _():
        prev_ref[...] = jnp.zeros_like(prev_ref)

    x = x_ref[...]
    xb = x.astype(BF16)

    def up(slot, j):
        cols = slice(j * tn, (j + 1) * tn)
        h_ref[slot, 0] = _dot(xb, wv_ref[:, cols])
        h_ref[slot, 1] = _dot(xb, wg_ref[:, cols])

    def conv(slot, part, j):
        cols = slice(part * d_ff + j * tn, part * d_ff + (j + 1) * tn)
        h = h_ref[slot, part]
        prev8 = prev_ref[:, cols]
        out = cb_ref[:, cols] + cw_ref[n_taps - 1:n_taps, cols] * h
        for k in range(1, n_taps):
            out = out + cw_ref[n_taps - 1 - k:n_taps - k, cols] * _shift_rows(prev8, h, k)
        prev_ref[:, cols] = h[tm - SUBLANES:, :]
        return out

    up(0, 0)
    for j in range(n_chunks):
        if j + 1 < n_chunks:
            up((j + 1) % 2, j + 1)
        val = conv(j % 2, 0, j)
        gate = conv(j % 2, 1, j)
        act_ref[:, j * tn:(j + 1) * tn] = (jax.nn.gelu(gate) * val).astype(BF16)

    ff = _dot(act_ref[...], wd_ref[...])
    x2 = _layer_norm(ALPHA * x + ff, g_ref[...], b_ref[...])
    gate = jax.nn.sigmoid(_dot(x2.astype(BF16), wpg_ref[...]) + bpg_ref[...])
    y_ref[...] = x2 + gate * _dot(p_ref[...].astype(BF16), wp_ref[...])


def _ffn_layer(x, w_up, conv_w, conv_b, w_down, g, b, p, ple_w, ple_gate_w, ple_gate_b):
    s, d = x.shape
    d_ff = w_down.shape[0]
    d_ple = p.shape[1]
    n_taps = conv_w.shape[0]
    assert s % TM_FFN == 0 and d_ff % TN_FFN == 0 and n_taps - 1 < SUBLANES
    row = lambda v: v.reshape(1, d)
    return pl.pallas_call(
        _ffn_kernel,
        grid=(s // TM_FFN,),
        in_specs=[
            pl.BlockSpec((TM_FFN, d), lambda i: (i, 0)),
            _resident((d, d_ff)), _resident((d, d_ff)),
            _resident((n_taps, 2 * d_ff)), _resident((1, 2 * d_ff)),
            _resident((d_ff, d)), _resident((1, d)), _resident((1, d)),
            pl.BlockSpec((TM_FFN, d_ple), lambda i: (i, 0)),
            _resident((d_ple, d)), _resident((d, d)), _resident((1, d)),
        ],
        out_specs=pl.BlockSpec((TM_FFN, d), lambda i: (i, 0)),
        out_shape=jax.ShapeDtypeStruct((s, d), F32),
        scratch_shapes=[pltpu.VMEM((SUBLANES, 2 * d_ff), F32),
                        pltpu.VMEM((2, 2, TM_FFN, TN_FFN), F32),
                        pltpu.VMEM((TM_FFN, d_ff), BF16)],
        compiler_params=pltpu.CompilerParams(
            dimension_semantics=("arbitrary",), vmem_limit_bytes=VMEM_LIMIT_BYTES),
        name="conv_ffn",
    )(x, w_up[:, :d_ff].astype(BF16), w_up[:, d_ff:].astype(BF16), conv_w,
      conv_b.reshape(1, 2 * d_ff), w_down.astype(BF16), row(g), row(b), p,
      ple_w.astype(BF16), ple_gate_w.astype(BF16), row(ple_gate_b))


def kernel(x, p, a_w_in, a_b_f, a_w_out, b_w_in, b_conv_w, b_conv_b, b_w_a, b_b_a, b_w_i, b_b_i,
           b_lam, b_w_out, f_w_up, f_conv_w, f_conv_b, f_w_down, ln1_g, ln1_b, ln2_g, ln2_b,
           ple_w, ple_gate_w, ple_gate_b):
    bsz, s, d = x.shape
    depth = p.shape[0]
    outs = []
    for bi in range(bsz):
        xs = x[bi]
        for i in range(depth):
            j = i // 2
            if i % 2 == 0:
                q_aug, k_aug, v = _attn_proj(xs, a_w_in[j], a_b_f[j])
                o = _flash(q_aug, k_aug, v)
                xs = _attn_out(o, a_w_out[j], xs, ln1_g[i], ln1_b[i])
            else:
                xs = _rglru_layer(xs, b_w_in[j], b_conv_w[j], b_conv_b[j], b_w_a[j], b_b_a[j],
                                  b_w_i[j], b_b_i[j], b_lam[j], b_w_out[j], ln1_g[i], ln1_b[i])
            xs = _ffn_layer(xs, f_w_up[i], f_conv_w[i], f_conv_b[i], f_w_down[i],
                            ln2_g[i], ln2_b[i], p[i, bi], ple_w[i], ple_gate_w[i], ple_gate_b[i])
        outs.append(xs)
    return jnp.stack(outs, axis=0)
```

```python
import functools
import math

import jax
import jax.numpy as jnp
from jax import lax
from jax.experimental import pallas as pl
from jax.experimental.pallas import tpu as pltpu

N_HEADS = 16
LRU_C = 8.0
LN_EPS = 1e-5
DEPTH = 4
ALPHA = (2.0 * DEPTH) ** 0.25
LOG2E = math.log2(math.e)

LANES = 128
SUBLANES = 8
VMEM_LIMIT_BYTES = 56 * 1024 * 1024

TQ = 2048
TK = 256
QSTRIP = 256
TM_PROJ = 512
TM_OUT = 512
TM_FFN = 512
TN_FFN = 256
TM_LRU = 256

AUG_WIDTH = 6
VT_ROWS = LANES // 2 + 2 * SUBLANES
NEG_BIG = -1e30

BF16 = jnp.bfloat16
F32 = jnp.float32
NT_DIMS = (((1,), (1,)), ((), ()))


def _resident(shape):
    n = len(shape)
    return pl.BlockSpec(shape, lambda *_: (0,) * n, pipeline_mode=pl.Buffered(1))


def _dot(a, b):
    return jnp.dot(a, b, preferred_element_type=F32)


def _layer_norm(y, g, b):
    mu = jnp.mean(y, axis=-1, keepdims=True)
    yc = y - mu
    var = jnp.mean(yc * yc, axis=-1, keepdims=True)
    return yc * lax.rsqrt(var + LN_EPS) * g + b


def _split3(x):
    p1 = x.astype(BF16)
    r1 = x - p1.astype(F32)
    p2 = r1.astype(BF16)
    p3 = (r1 - p2.astype(F32)).astype(BF16)
    return p1, p2, p3


def _shift_rows(prev8, cur, k):
    both = jnp.concatenate([prev8, cur], axis=0)
    return pltpu.roll(both, k, 0)[SUBLANES:]


def _attn_proj_kernel(x_ref, wq_ref, wk_ref, wvt_ref, wf_ref, bf_ref, tri_ref, pq_ref, pk_ref,
                      q_ref, k_ref, vt_ref, carry_ref, *, n_heads, scale):
    tm = x_ref.shape[0]

    @pl.when(pl.program_id(0) == 0)
    def _():
        carry_ref[...] = jnp.zeros_like(carry_ref)

    xb = x_ref[...].astype(BF16)
    lane = lax.broadcasted_iota(jnp.int32, (tm, LANES), 1)

    fg = _dot(xb, wf_ref[...]) + bf_ref[...]
    lf = (jnp.minimum(fg, 0.0) - jnp.log1p(jnp.exp(-jnp.abs(fg)))) * LOG2E
    lf = jnp.where(lane < n_heads, lf, 0.0)

    cs = _dot(tri_ref[...], jnp.concatenate(_split3(lf), axis=1))
    c = cs[:, :LANES] + cs[:, LANES:2 * LANES] + cs[:, 2 * LANES:] + carry_ref[0:1, :]
    carry_ref[...] = jnp.broadcast_to(c[tm - 1:tm, :], carry_ref.shape)

    c1, c2, c3 = _split3(c)
    cz = (c1.astype(F32) + pltpu.roll(c2.astype(F32), n_heads, 1)
          + pltpu.roll(c3.astype(F32), 2 * n_heads, 1))
    cz = jnp.where(lane == 3 * n_heads, 1.0, cz).astype(BF16)
    eq = _dot(cz, pq_ref[...])
    ek = _dot(cz, pk_ref[...])

    qf = _dot(xb, wq_ref[...]) * (scale * LOG2E)
    kf = _dot(xb, wk_ref[...])
    half = LANES // 2
    for pair in range(n_heads // 2):
        cols = slice(pair * LANES, (pair + 1) * LANES)
        q_ref[2 * pair] = jnp.where(lane < half, qf[:, cols], eq[:, cols]).astype(BF16)
        q_ref[2 * pair + 1] = jnp.where(lane >= half, qf[:, cols], eq[:, cols]).astype(BF16)
        k_ref[2 * pair] = jnp.where(lane < half, kf[:, cols], ek[:, cols]).astype(BF16)
        k_ref[2 * pair + 1] = jnp.where(lane >= half, kf[:, cols], ek[:, cols]).astype(BF16)

    vt = lax.dot_general(wvt_ref[...], xb, NT_DIMS, preferred_element_type=F32).astype(BF16)
    dh = LANES // 2
    tk = vt_ref.shape[3]
    ones_rows = jnp.where(lax.broadcasted_iota(jnp.int32, (VT_ROWS - dh, tk), 0) == 0, 1.0, 0.0)
    for blk in range(tm // tk):
        for h in range(n_heads):
            vt_ref[blk, h, :dh, :] = vt[h * dh:(h + 1) * dh, blk * tk:(blk + 1) * tk]
            vt_ref[blk, h, dh:, :] = ones_rows.astype(BF16)


def _placement_matrices(n_heads):
    import numpy as np
    pq = np.zeros((LANES, n_heads // 2 * LANES), np.float32)
    pk = np.zeros((LANES, n_heads // 2 * LANES), np.float32)
    one_lane = 3 * n_heads
    for h in range(n_heads):
        base = (h // 2) * LANES + (LANES // 2 if h % 2 == 0 else 0)
        for piece in range(3):
            pq[piece * n_heads + h, base + piece] = 1.0
            pq[one_lane, base + 3 + piece] = 1.0
            pk[one_lane, base + piece] = 1.0
            pk[piece * n_heads + h, base + 3 + piece] = -1.0
    return jnp.asarray(pq, BF16), jnp.asarray(pk, BF16)


def _attn_proj(x, w_in, b_f):
    s, d = x.shape
    n_heads = b_f.shape[0]
    dh = d // n_heads
    assert 2 * dh == LANES and 3 * n_heads < LANES and s % TM_PROJ == 0
    wq = w_in[:, :d].astype(BF16)
    wk = w_in[:, d:2 * d].astype(BF16)
    wvt = w_in[:, 2 * d:3 * d].T.astype(BF16)
    wf = jnp.pad(w_in[:, 3 * d:], ((0, 0), (0, LANES - n_heads))).astype(BF16)
    bf = jnp.pad(b_f, (0, LANES - n_heads)).reshape(1, LANES)
    tri = jnp.tril(jnp.ones((TM_PROJ, TM_PROJ), BF16))
    pq, pk = _placement_matrices(n_heads)
    kern = functools.partial(_attn_proj_kernel, n_heads=n_heads, scale=1.0 / math.sqrt(dh))
    return pl.pallas_call(
        kern,
        grid=(s // TM_PROJ,),
        in_specs=[
            pl.BlockSpec((TM_PROJ, d), lambda i: (i, 0)),
            _resident((d, d)), _resident((d, d)), _resident((d, d)),
            _resident((d, LANES)), _resident((1, LANES)),
            _resident((TM_PROJ, TM_PROJ)),
            _resident(pq.shape), _resident(pk.shape),
        ],
        out_specs=[
            pl.BlockSpec((n_heads, TM_PROJ, LANES), lambda i: (0, i, 0)),
            pl.BlockSpec((n_heads, TM_PROJ, LANES), lambda i: (0, i, 0)),
            pl.BlockSpec((TM_PROJ // TK, n_heads, VT_ROWS, TK), lambda i: (i, 0, 0, 0)),
        ],
        out_shape=[
            jax.ShapeDtypeStruct((n_heads, s, LANES), BF16),
            jax.ShapeDtypeStruct((n_heads, s, LANES), BF16),
            jax.ShapeDtypeStruct((s // TK, n_heads, VT_ROWS, TK), BF16),
        ],
        scratch_shapes=[pltpu.VMEM((SUBLANES, LANES), F32)],
        compiler_params=pltpu.CompilerParams(
            dimension_semantics=("arbitrary",), vmem_limit_bytes=VMEM_LIMIT_BYTES),
        name="attn_proj",
    )(x, wq, wk, wvt, wf, bf, tri, pq, pk)


def _flash_kernel(q_ref, k_ref, vt_ref, o_ref, s_ref):
    tq = q_ref.shape[1]
    tk = vt_ref.shape[3]
    dh = LANES // 2
    qi = pl.program_id(1)
    qs = (q_ref[0], q_ref[1])

    def scores(slot, j, c0=0):
        for hh in range(2):
            k = k_ref[hh, pl.ds(pl.multiple_of(j * tk, tk), tk), :]
            s_ref[slot, hh, :, c0:] = lax.dot_general(k, qs[hh][c0:], NT_DIMS,
                                                      preferred_element_type=F32)

    def update(slot, j, stats, c0=0, mask=None):
        out = []
        for hh in range(2):
            m, acc = stats[hh]
            ms, accs = [m[:, :c0]] if c0 else [], [acc[:, :c0]] if c0 else []
            for q0 in range(c0, tq, QSTRIP):
                cols = slice(q0, q0 + QSTRIP)
                s = s_ref[slot, hh, :, cols]
                if mask is not None:
                    s = jnp.where(mask[:, q0 - c0:q0 - c0 + QSTRIP], s, NEG_BIG)
                m_new = jnp.maximum(m[:, cols], jnp.max(s, axis=0, keepdims=True))
                p = jnp.exp2(s - m_new).astype(BF16)
                accs.append(jnp.exp2(m[:, cols] - m_new) * acc[:, cols] + _dot(vt_ref[j, hh], p))
                ms.append(m_new)
            out.append((jnp.concatenate(ms, axis=1), jnp.concatenate(accs, axis=1)))
        return tuple(out)

    n_sub = tq // tk

    def body(jj, stats):
        for b in range(n_sub):
            scores((b + 1) % 2, n_sub * jj + b + 1)
            stats = update(b % 2, n_sub * jj + b, stats)
        return stats

    init = (jnp.full((1, tq), NEG_BIG, F32), jnp.zeros((VT_ROWS, tq), F32))
    scores(0, 0)
    stats = lax.fori_loop(0, qi, body, (init, init))
    for b in range(n_sub):
        if b + 1 < n_sub:
            scores((b + 1) % 2, n_sub * qi + b + 1, (b + 1) * tk)
        key = lax.broadcasted_iota(jnp.int32, (tk, tq - b * tk), 0)
        qry = lax.broadcasted_iota(jnp.int32, (tk, tq - b * tk), 1)
        stats = update(b % 2, n_sub * qi + b, stats, b * tk, key <= qry)
    outs = [acc[:dh] / acc[dh:dh + 1] for _, acc in stats]
    o_ref[...] = jnp.concatenate(outs, axis=0).T.astype(o_ref.dtype)


def _flash(q_aug, k_aug, vt):
    n_heads, s, _ = q_aug.shape
    n_kv, _, vt_rows, tk = vt.shape
    d = n_heads * (LANES // 2)
    assert TQ % (2 * tk) == 0 and s % TQ == 0
    return pl.pallas_call(
        _flash_kernel,
        grid=(n_heads // 2, s // TQ),
        in_specs=[
            pl.BlockSpec((2, TQ, LANES), lambda hp, qi: (hp, qi, 0)),
            pl.BlockSpec((2, s, LANES), lambda hp, qi: (hp, 0, 0)),
            pl.BlockSpec((n_kv, 2, vt_rows, tk), lambda hp, qi: (0, hp, 0, 0)),
        ],
        out_specs=pl.BlockSpec((TQ, LANES), lambda hp, qi: (qi, hp)),
        out_shape=jax.ShapeDtypeStruct((s, d), BF16),
        scratch_shapes=[pltpu.VMEM((2, 2, tk, TQ), F32)],
        compiler_params=pltpu.CompilerParams(
            dimension_semantics=("arbitrary", "arbitrary"), vmem_limit_bytes=VMEM_LIMIT_BYTES),
        name="flash_attn",
    )(q_aug, k_aug, vt)


def _attn_out_kernel(o_ref, w_ref, x_ref, g_ref, b_ref, y_ref):
    m = _dot(o_ref[...], w_ref[...])
    y_ref[...] = _layer_norm(ALPHA * x_ref[...] + m, g_ref[...], b_ref[...])


def _attn_out(o, w_out, x, g, b):
    s, d = x.shape
    assert s % TM_OUT == 0
    return pl.pallas_call(
        _attn_out_kernel,
        grid=(s // TM_OUT,),
        in_specs=[
            pl.BlockSpec((TM_OUT, d), lambda i: (i, 0)),
            _resident((d, d)),
            pl.BlockSpec((TM_OUT, d), lambda i: (i, 0)),
            _resident((1, d)), _resident((1, d)),
        ],
        out_specs=pl.BlockSpec((TM_OUT, d), lambda i: (i, 0)),
        out_shape=jax.ShapeDtypeStruct((s, d), F32),
        compiler_params=pltpu.CompilerParams(
            dimension_semantics=("arbitrary",), vmem_limit_bytes=VMEM_LIMIT_BYTES),
        name="attn_out",
    )(o, w_out.astype(BF16), x, g.reshape(1, d), b.reshape(1, d))


def _rglru_kernel(x_ref, wx_ref, wg_ref, cw_ref, cb_ref, wa_ref, ba_ref, wi_ref, bi_ref,
                  lam_ref, wo_ref, g_ref, b_ref, y_ref, xprev_ref, hprev_ref):
    tm, d = x_ref.shape
    n_blocks = wa_ref.shape[0]
    blk = d // n_blocks
    n_taps = cw_ref.shape[0]

    @pl.when(pl.program_id(0) == 0)
    def _():
        xprev_ref[...] = jnp.zeros_like(xprev_ref)
        hprev_ref[...] = jnp.zeros_like(hprev_ref)

    x = x_ref[...]
    xb = x.astype(BF16)
    xp = _dot(xb, wx_ref[...])
    gb = _dot(xb, wg_ref[...])

    prev8 = xprev_ref[...]
    xc = cb_ref[...] + cw_ref[n_taps - 1:n_taps, :] * xp
    for k in range(1, n_taps):
        xc = xc + cw_ref[n_taps - 1 - k:n_taps - k, :] * _shift_rows(prev8, xp, k)
    xprev_ref[...] = xp[tm - SUBLANES:, :]

    r_parts, i_parts = [], []
    for n in range(n_blocks):
        xh = xc[:, n * blk:(n + 1) * blk].astype(BF16)
        r_parts.append(_dot(xh, wa_ref[n]))
        i_parts.append(_dot(xh, wi_ref[n]))
    r = jax.nn.sigmoid(jnp.concatenate(r_parts, axis=1) + ba_ref[...])
    ig = jax.nn.sigmoid(jnp.concatenate(i_parts, axis=1) + bi_ref[...])

    log_a = (-LRU_C) * r * jax.nn.softplus(-lam_ref[...])
    a = jnp.exp(log_a)
    u = jnp.sqrt(-jnp.tanh(log_a) * (a * a + 1.0)) * (ig * xc)

    row = lax.broadcasted_iota(jnp.int32, (tm, d), 0)
    dist = 1
    while dist < tm:
        if dist < SUBLANES:
            a_sh = jnp.where(row >= dist, pltpu.roll(a, dist, 0), 1.0)
            u_sh = jnp.where(row >= dist, pltpu.roll(u, dist, 0), 0.0)
        else:
            a_sh = jnp.concatenate([jnp.ones((dist, d), F32), a[:tm - dist]], axis=0)
            u_sh = jnp.concatenate([jnp.zeros((dist, d), F32), u[:tm - dist]], axis=0)
        u = u + a * u_sh
        a = a * a_sh
        dist *= 2
    h = u + a * hprev_ref[0:1, :]
    hprev_ref[...] = jnp.broadcast_to(h[tm - 1:tm, :], hprev_ref.shape)

    yv = (h * jax.nn.gelu(gb)).astype(BF16)
    m = _dot(yv, wo_ref[...])
    y_ref[...] = _layer_norm(ALPHA * x + m, g_ref[...], b_ref[...])


def _rglru_layer(x, w_in, conv_w, conv_b, w_a, b_a, w_i, b_i, lam, w_out, g, b):
    s, d = x.shape
    n_blocks, blk, _ = w_a.shape
    assert s % TM_LRU == 0 and conv_w.shape[0] - 1 < SUBLANES
    row = lambda v: v.reshape(1, d)
    return pl.pallas_call(
        _rglru_kernel,
        grid=(s // TM_LRU,),
        in_specs=[
            pl.BlockSpec((TM_LRU, d), lambda i: (i, 0)),
            _resident((d, d)), _resident((d, d)),
            _resident(conv_w.shape), _resident((1, d)),
            _resident(w_a.shape), _resident((1, d)),
            _resident(w_i.shape), _resident((1, d)),
            _resident((1, d)),
            _resident((d, d)), _resident((1, d)), _resident((1, d)),
        ],
        out_specs=pl.BlockSpec((TM_LRU, d), lambda i: (i, 0)),
        out_shape=jax.ShapeDtypeStruct((s, d), F32),
        scratch_shapes=[pltpu.VMEM((SUBLANES, d), F32), pltpu.VMEM((SUBLANES, d), F32)],
        compiler_params=pltpu.CompilerParams(
            dimension_semantics=("arbitrary",), vmem_limit_bytes=VMEM_LIMIT_BYTES),
        name="rglru",
    )(x, w_in[:, :d].astype(BF16), w_in[:, d:].astype(BF16), conv_w, row(conv_b),
      w_a.astype(BF16), row(b_a), w_i.astype(BF16), row(b_i), row(lam),
      w_out.astype(BF16), row(g), row(b))


def _ffn_kernel(x_ref, wv_ref, wg_ref, cw_ref, cb_ref, wd_ref, g_ref, b_ref,
                p_ref, wp_ref, wpg_ref, bpg_ref, y_ref, prev_ref, h_ref, act_ref):
    tm, d = x_ref.shape
    d_ff = wd_ref.shape[0]
    tn = h_ref.shape[3]
    n_chunks = d_ff // tn
    n_taps = cw_ref.shape[0]

    @pl.when(pl.program_id(0) == 0)
    def _():
        prev_ref[...] = jnp.zeros_like(prev_ref)

    x = x_ref[...]
    xb = x.astype(BF16)

    def up(slot, j):
        cols = slice(j * tn, (j + 1) * tn)
        h_ref[slot, 0] = _dot(xb, wv_ref[:, cols])
        h_ref[slot, 1] = _dot(xb, wg_ref[:, cols])

    def conv(slot, part, j):
        cols = slice(part * d_ff + j * tn, part * d_ff + (j + 1) * tn)
        h = h_ref[slot, part]
        prev8 = prev_ref[:, cols]
        out = cb_ref[:, cols] + cw_ref[n_taps - 1:n_taps, cols] * h
        for k in range(1, n_taps):
            out = out + cw_ref[n_taps - 1 - k:n_taps - k, cols] * _shift_rows(prev8, h, k)
        prev_ref[:, cols] = h[tm - SUBLANES:, :]
        return out

    up(0, 0)
    for j in range(n_chunks):
        if j + 1 < n_chunks:
            up((j + 1) % 2, j + 1)
        val = conv(j % 2, 0, j)
        gate = conv(j % 2, 1, j)
        act_ref[:, j * tn:(j + 1) * tn] = (jax.nn.gelu(gate) * val).astype(BF16)

    ff = _dot(act_ref[...], wd_ref[...])
    x2 = _layer_norm(ALPHA * x + ff, g_ref[...], b_ref[...])
    gate = jax.nn.sigmoid(_dot(x2.astype(BF16), wpg_ref[...]) + bpg_ref[...])
    y_ref[...] = x2 + gate * _dot(p_ref[...].astype(BF16), wp_ref[...])


def _ffn_layer(x, w_up, conv_w, conv_b, w_down, g, b, p, ple_w, ple_gate_w, ple_gate_b):
    s, d = x.shape
    d_ff = w_down.shape[0]
    d_ple = p.shape[1]
    n_taps = conv_w.shape[0]
    assert s % TM_FFN == 0 and d_ff % TN_FFN == 0 and n_taps - 1 < SUBLANES
    row = lambda v: v.reshape(1, d)
    return pl.pallas_call(
        _ffn_kernel,
        grid=(s // TM_FFN,),
        in_specs=[
            pl.BlockSpec((TM_FFN, d), lambda i: (i, 0)),
            _resident((d, d_ff)), _resident((d, d_ff)),
            _resident((n_taps, 2 * d_ff)), _resident((1, 2 * d_ff)),
            _resident((d_ff, d)), _resident((1, d)), _resident((1, d)),
            pl.BlockSpec((TM_FFN, d_ple), lambda i: (i, 0)),
            _resident((d_ple, d)), _resident((d, d)), _resident((1, d)),
        ],
        out_specs=pl.BlockSpec((TM_FFN, d), lambda i: (i, 0)),
        out_shape=jax.ShapeDtypeStruct((s, d), F32),
        scratch_shapes=[pltpu.VMEM((SUBLANES, 2 * d_ff), F32),
                        pltpu.VMEM((2, 2, TM_FFN, TN_FFN), F32),
                        pltpu.VMEM((TM_FFN, d_ff), BF16)],
        compiler_params=pltpu.CompilerParams(
            dimension_semantics=("arbitrary",), vmem_limit_bytes=VMEM_LIMIT_BYTES),
        name="conv_ffn",
    )(x, w_up[:, :d_ff].astype(BF16), w_up[:, d_ff:].astype(BF16), conv_w,
      conv_b.reshape(1, 2 * d_ff), w_down.astype(BF16), row(g), row(b), p,
      ple_w.astype(BF16), ple_gate_w.astype(BF16), row(ple_gate_b))


def kernel(x, p, a_w_in, a_b_f, a_w_out, b_w_in, b_conv_w, b_conv_b, b_w_a, b_b_a, b_w_i, b_b_i,
           b_lam, b_w_out, f_w_up, f_conv_w, f_conv_b, f_w_down, ln1_g, ln1_b, ln2_g, ln2_b,
           ple_w, ple_gate_w, ple_gate_b):
    bsz, s, d = x.shape
    depth = p.shape[0]
    outs = []
    for bi in range(bsz):
        xs = x[bi]
        for i in range(depth):
            j = i // 2
            if i % 2 == 0:
                q_aug, k_aug, v = _attn_proj(xs, a_w_in[j], a_b_f[j])
                o = _flash(q_aug, k_aug, v)
                xs = _attn_out(o, a_w_out[j], xs, ln1_g[i], ln1_b[i])
            else:
                xs = _rglru_layer(xs, b_w_in[j], b_conv_w[j], b_conv_b[j], b_w_a[j], b_b_a[j],
                                  b_w_i[j], b_b_i[j], b_lam[j], b_w_out[j], ln1_g[i], ln1_b[i])
            xs = _ffn_layer(xs, f_w_up[i], f_conv_w[i], f_conv_b[i], f_w_down[i],
                            ln2_g[i], ln2_b[i], p[i, bi], ple_w[i], ple_gate_w[i], ple_gate_b[i])
        outs.append(xs)
    return jnp.stack(outs, axis=0)
```

```python
import functools
import math

import jax
import jax.numpy as jnp
from jax import lax
from jax.experimental import pallas as pl
from jax.experimental.pallas import tpu as pltpu

N_HEADS = 16
LRU_C = 8.0
LN_EPS = 1e-5
DEPTH = 4
ALPHA = (2.0 * DEPTH) ** 0.25
LOG2E = math.log2(math.e)

LANES = 128
SUBLANES = 8
VMEM_LIMIT_BYTES = 56 * 1024 * 1024

TQ = 2048
TK = 256
QSTRIP = 256
TM_PROJ = 512
TM_OUT = 512
TM_FFN = 512
TN_FFN = 256
TM_LRU = 256

AUG_WIDTH = 6
VT_ROWS = LANES // 2 + 2 * SUBLANES
NEG_BIG = -1e30

BF16 = jnp.bfloat16
F32 = jnp.float32
NT_DIMS = (((1,), (1,)), ((), ()))


def _resident(shape):
    n = len(shape)
    return pl.BlockSpec(shape, lambda *_: (0,) * n, pipeline_mode=pl.Buffered(1))


def _layer_block(block, *index):
    return pl.BlockSpec((None,) + tuple(block), lambda *_: tuple(index),
                        pipeline_mode=pl.Buffered(1))


def _rows(v):
    return v.reshape(v.shape[0], 1, -1)


def _dot(a, b):
    return jnp.dot(a, b, preferred_element_type=F32)


def _layer_norm(y, g, b):
    mu = jnp.mean(y, axis=-1, keepdims=True)
    yc = y - mu
    var = jnp.mean(yc * yc, axis=-1, keepdims=True)
    return yc * lax.rsqrt(var + LN_EPS) * g + b


def _split3(x):
    p1 = x.astype(BF16)
    r1 = x - p1.astype(F32)
    p2 = r1.astype(BF16)
    p3 = (r1 - p2.astype(F32)).astype(BF16)
    return p1, p2, p3


def _shift_rows(prev8, cur, k):
    both = jnp.concatenate([prev8, cur], axis=0)
    return pltpu.roll(both, k, 0)[SUBLANES:]


def _attn_proj_kernel(x_ref, wq_ref, wk_ref, wvt_ref, wf_ref, bf_ref, tri_ref, pq_ref, pk_ref,
                      q_ref, k_ref, vt_ref, carry_ref, *, n_heads, scale):
    tm = x_ref.shape[0]

    @pl.when(pl.program_id(0) == 0)
    def _():
        carry_ref[...] = jnp.zeros_like(carry_ref)

    xb = x_ref[...].astype(BF16)
    lane = lax.broadcasted_iota(jnp.int32, (tm, LANES), 1)

    fg = _dot(xb, wf_ref[...]) + bf_ref[...]
    lf = (jnp.minimum(fg, 0.0) - jnp.log1p(jnp.exp(-jnp.abs(fg)))) * LOG2E
    lf = jnp.where(lane < n_heads, lf, 0.0)

    cs = _dot(tri_ref[...], jnp.concatenate(_split3(lf), axis=1))
    c = cs[:, :LANES] + cs[:, LANES:2 * LANES] + cs[:, 2 * LANES:] + carry_ref[0:1, :]
    carry_ref[...] = jnp.broadcast_to(c[tm - 1:tm, :], carry_ref.shape)

    c1, c2, c3 = _split3(c)
    cz = (c1.astype(F32) + pltpu.roll(c2.astype(F32), n_heads, 1)
          + pltpu.roll(c3.astype(F32), 2 * n_heads, 1))
    cz = jnp.where(lane == 3 * n_heads, 1.0, cz).astype(BF16)
    eq = _dot(cz, pq_ref[...])
    ek = _dot(cz, pk_ref[...])

    qf = _dot(xb, wq_ref[...]) * (scale * LOG2E)
    kf = _dot(xb, wk_ref[...])
    half = LANES // 2
    for pair in range(n_heads // 2):
        cols = slice(pair * LANES, (pair + 1) * LANES)
        q_ref[2 * pair] = jnp.where(lane < half, qf[:, cols], eq[:, cols]).astype(BF16)
        q_ref[2 * pair + 1] = jnp.where(lane >= half, qf[:, cols], eq[:, cols]).astype(BF16)
        k_ref[2 * pair] = jnp.where(lane < half, kf[:, cols], ek[:, cols]).astype(BF16)
        k_ref[2 * pair + 1] = jnp.where(lane >= half, kf[:, cols], ek[:, cols]).astype(BF16)

    vt = lax.dot_general(wvt_ref[...], xb, NT_DIMS, preferred_element_type=F32).astype(BF16)
    dh = LANES // 2
    tk = vt_ref.shape[3]
    ones_rows = jnp.where(lax.broadcasted_iota(jnp.int32, (VT_ROWS - dh, tk), 0) == 0, 1.0, 0.0)
    for blk in range(tm // tk):
        for h in range(n_heads):
            vt_ref[blk, h, :dh, :] = vt[h * dh:(h + 1) * dh, blk * tk:(blk + 1) * tk]
            vt_ref[blk, h, dh:, :] = ones_rows.astype(BF16)


def _placement_matrices(n_heads):
    import numpy as np
    pq = np.zeros((LANES, n_heads // 2 * LANES), np.float32)
    pk = np.zeros((LANES, n_heads // 2 * LANES), np.float32)
    one_lane = 3 * n_heads
    for h in range(n_heads):
        base = (h // 2) * LANES + (LANES // 2 if h % 2 == 0 else 0)
        for piece in range(3):
            pq[piece * n_heads + h, base + piece] = 1.0
            pq[one_lane, base + 3 + piece] = 1.0
            pk[one_lane, base + piece] = 1.0
            pk[piece * n_heads + h, base + 3 + piece] = -1.0
    return jnp.asarray(pq, BF16), jnp.asarray(pk, BF16)


def _attn_weights(a_w_in, a_b_f):
    n_a, d, _ = a_w_in.shape
    n_heads = a_b_f.shape[1]
    wvt = jnp.swapaxes(a_w_in[:, :, 2 * d:3 * d], 1, 2).astype(BF16)
    wf = jnp.pad(a_w_in[:, :, 3 * d:], ((0, 0), (0, 0), (0, LANES - n_heads))).astype(BF16)
    bf = jnp.pad(a_b_f, ((0, 0), (0, LANES - n_heads))).reshape(n_a, 1, LANES)
    return a_w_in.astype(BF16), wvt, wf, bf


def _attn_proj(x, weights, j, n_heads):
    w_in, wvt, wf, bf = weights
    s, d = x.shape
    dh = d // n_heads
    assert 2 * dh == LANES and 3 * n_heads < LANES and s % TM_PROJ == 0
    tri = jnp.tril(jnp.ones((TM_PROJ, TM_PROJ), BF16))
    pq, pk = _placement_matrices(n_heads)
    kern = functools.partial(_attn_proj_kernel, n_heads=n_heads, scale=1.0 / math.sqrt(dh))
    return pl.pallas_call(
        kern,
        grid=(s // TM_PROJ,),
        in_specs=[
            pl.BlockSpec((TM_PROJ, d), lambda i: (i, 0)),
            _layer_block((d, d), j, 0, 0), _layer_block((d, d), j, 0, 1),
            _layer_block((d, d), j, 0, 0),
            _layer_block((d, LANES), j, 0, 0), _layer_block((1, LANES), j, 0, 0),
            _resident((TM_PROJ, TM_PROJ)),
            _resident(pq.shape), _resident(pk.shape),
        ],
        out_specs=[
            pl.BlockSpec((n_heads, TM_PROJ, LANES), lambda i: (0, i, 0)),
            pl.BlockSpec((n_heads, TM_PROJ, LANES), lambda i: (0, i, 0)),
            pl.BlockSpec((TM_PROJ // TK, n_heads, VT_ROWS, TK), lambda i: (i, 0, 0, 0)),
        ],
        out_shape=[
            jax.ShapeDtypeStruct((n_heads, s, LANES), BF16),
            jax.ShapeDtypeStruct((n_heads, s, LANES), BF16),
            jax.ShapeDtypeStruct((s // TK, n_heads, VT_ROWS, TK), BF16),
        ],
        scratch_shapes=[pltpu.VMEM((SUBLANES, LANES), F32)],
        compiler_params=pltpu.CompilerParams(
            dimension_semantics=("arbitrary",), vmem_limit_bytes=VMEM_LIMIT_BYTES),
        name="attn_proj",
    )(x, w_in, w_in, wvt, wf, bf, tri, pq, pk)


def _flash_kernel(q_ref, k_ref, vt_ref, o_ref, s_ref):
    tq = q_ref.shape[1]
    tk = vt_ref.shape[3]
    dh = LANES // 2
    qi = pl.program_id(1)
    qs = (q_ref[0], q_ref[1])

    tri = (lax.broadcasted_iota(jnp.int32, (tk, QSTRIP), 0)
           <= lax.broadcasted_iota(jnp.int32, (tk, QSTRIP), 1))

    def scores(slot, j, c0=0, masked=False):
        maxima = []
        for hh in range(2):
            k = k_ref[hh, pl.ds(pl.multiple_of(j * tk, tk), tk), :]
            s = lax.dot_general(k, qs[hh][c0:], NT_DIMS, preferred_element_type=F32)
            if masked:
                parts = [jnp.where(tri, s[:, :QSTRIP], NEG_BIG)]
                if tq - c0 > QSTRIP:
                    parts.append(s[:, QSTRIP:])
                s = jnp.concatenate(parts, axis=1)
            s_ref[slot, hh, :, c0:] = s
            maxima.append(jnp.max(s, axis=0, keepdims=True))
        return tuple(maxima)

    def update(slot, j, stats, maxima, c0=0, masked=False):
        out = []
        for hh in range(2):
            m, acc = stats[hh]
            ms, accs = [m[:, :c0]] if c0 else [], [acc[:, :c0]] if c0 else []
            for q0 in range(c0, tq, QSTRIP):
                cols = slice(q0, q0 + QSTRIP)
                s = s_ref[slot, hh, :, cols]
                if masked and q0 == c0:
                    s = jnp.where(tri, s, NEG_BIG)
                    s_max = jnp.max(s, axis=0, keepdims=True)
                else:
                    s_max = maxima[hh][:, q0 - c0:q0 - c0 + QSTRIP]
                m_new = jnp.maximum(m[:, cols], s_max)
                p = jnp.exp2(s - m_new).astype(BF16)
                accs.append(jnp.exp2(m[:, cols] - m_new) * acc[:, cols] + _dot(vt_ref[j, hh], p))
                ms.append(m_new)
            out.append((jnp.concatenate(ms, axis=1), jnp.concatenate(accs, axis=1)))
        return tuple(out)

    n_sub = tq // tk

    def body(jj, carry):
        stats, maxima = carry
        for b in range(n_sub):
            next_maxima = scores((b + 1) % 2, n_sub * jj + b + 1)
            stats = update(b % 2, n_sub * jj + b, stats, maxima)
            maxima = next_maxima
        return stats, maxima

    init = (jnp.full((1, tq), NEG_BIG, F32), jnp.zeros((VT_ROWS, tq), F32))
    stats, maxima = lax.fori_loop(0, qi, body, ((init, init), scores(0, 0)))
    for b in range(n_sub):
        if b + 1 < n_sub:
            next_maxima = scores((b + 1) % 2, n_sub * qi + b + 1, (b + 1) * tk, masked=True)
        stats = update(b % 2, n_sub * qi + b, stats, maxima, b * tk, masked=(b == 0))
        maxima = next_maxima
    outs = [acc[:dh] / acc[dh:dh + 1] for _, acc in stats]
    o_ref[...] = jnp.concatenate(outs, axis=0).T.astype(o_ref.dtype)


def _flash(q_aug, k_aug, vt):
    n_heads, s, _ = q_aug.shape
    n_kv, _, vt_rows, tk = vt.shape
    d = n_heads * (LANES // 2)
    assert TQ % (2 * tk) == 0 and s % TQ == 0 and tk == QSTRIP
    return pl.pallas_call(
        _flash_kernel,
        grid=(n_heads // 2, s // TQ),
        in_specs=[
            pl.BlockSpec((2, TQ, LANES), lambda hp, qi: (hp, qi, 0)),
            pl.BlockSpec((2, s, LANES), lambda hp, qi: (hp, 0, 0)),
            pl.BlockSpec((n_kv, 2, vt_rows, tk), lambda hp, qi: (0, hp, 0, 0)),
        ],
        out_specs=pl.BlockSpec((TQ, LANES), lambda hp, qi: (qi, hp)),
        out_shape=jax.ShapeDtypeStruct((s, d), BF16),
        scratch_shapes=[pltpu.VMEM((2, 2, tk, TQ), F32)],
        compiler_params=pltpu.CompilerParams(
            dimension_semantics=("arbitrary", "arbitrary"), vmem_limit_bytes=VMEM_LIMIT_BYTES),
        name="flash_attn",
    )(q_aug, k_aug, vt)


def _attn_out_kernel(o_ref, w_ref, x_ref, g_ref, b_ref, y_ref):
    m = _dot(o_ref[...], w_ref[...])
    y_ref[...] = _layer_norm(ALPHA * x_ref[...] + m, g_ref[...], b_ref[...])


def _attn_out(o, w_out, j, x, g, b, i):
    s, d = x.shape
    assert s % TM_OUT == 0
    return pl.pallas_call(
        _attn_out_kernel,
        grid=(s // TM_OUT,),
        in_specs=[
            pl.BlockSpec((TM_OUT, d), lambda r: (r, 0)),
            _layer_block((d, d), j, 0, 0),
            pl.BlockSpec((TM_OUT, d), lambda r: (r, 0)),
            _layer_block((1, d), i, 0, 0), _layer_block((1, d), i, 0, 0),
        ],
        out_specs=pl.BlockSpec((TM_OUT, d), lambda i: (i, 0)),
        out_shape=jax.ShapeDtypeStruct((s, d), F32),
        compiler_params=pltpu.CompilerParams(
            dimension_semantics=("arbitrary",), vmem_limit_bytes=VMEM_LIMIT_BYTES),
        name="attn_out",
    )(o, w_out, x, g, b)


def _rglru_kernel(x_ref, wx_ref, wg_ref, cw_ref, cb_ref, wa_ref, ba_ref, wi_ref, bi_ref,
                  lam_ref, wo_ref, g_ref, b_ref, y_ref, xprev_ref, hprev_ref):
    tm, d = x_ref.shape
    n_blocks = wa_ref.shape[0]
    blk = d // n_blocks
    n_taps = cw_ref.shape[0]

    @pl.when(pl.program_id(0) == 0)
    def _():
        xprev_ref[...] = jnp.zeros_like(xprev_ref)
        hprev_ref[...] = jnp.zeros_like(hprev_ref)

    x = x_ref[...]
    xb = x.astype(BF16)
    xp = _dot(xb, wx_ref[...])
    gb = _dot(xb, wg_ref[...])

    prev8 = xprev_ref[...]
    xc = cb_ref[...] + cw_ref[n_taps - 1:n_taps, :] * xp
    for k in range(1, n_taps):
        xc = xc + cw_ref[n_taps - 1 - k:n_taps - k, :] * _shift_rows(prev8, xp, k)
    xprev_ref[...] = xp[tm - SUBLANES:, :]

    r_parts, i_parts = [], []
    for n in range(n_blocks):
        xh = xc[:, n * blk:(n + 1) * blk].astype(BF16)
        r_parts.append(_dot(xh, wa_ref[n]))
        i_parts.append(_dot(xh, wi_ref[n]))
    r = jax.nn.sigmoid(jnp.concatenate(r_parts, axis=1) + ba_ref[...])
    ig = jax.nn.sigmoid(jnp.concatenate(i_parts, axis=1) + bi_ref[...])

    log_a = (-LRU_C) * r * jax.nn.softplus(-lam_ref[...])
    a = jnp.exp(log_a)
    u = jnp.sqrt(-jnp.tanh(log_a) * (a * a + 1.0)) * (ig * xc)

    row = lax.broadcasted_iota(jnp.int32, (tm, d), 0)
    dist = 1
    while dist < tm:
        if dist < SUBLANES:
            a_sh = jnp.where(row >= dist, pltpu.roll(a, dist, 0), 1.0)
            u_sh = jnp.where(row >= dist, pltpu.roll(u, dist, 0), 0.0)
        else:
            a_sh = jnp.concatenate([jnp.ones((dist, d), F32), a[:tm - dist]], axis=0)
            u_sh = jnp.concatenate([jnp.zeros((dist, d), F32), u[:tm - dist]], axis=0)
        u = u + a * u_sh
        a = a * a_sh
        dist *= 2
    h = u + a * hprev_ref[0:1, :]
    hprev_ref[...] = jnp.broadcast_to(h[tm - 1:tm, :], hprev_ref.shape)

    yv = (h * jax.nn.gelu(gb)).astype(BF16)
    m = _dot(yv, wo_ref[...])
    y_ref[...] = _layer_norm(ALPHA * x + m, g_ref[...], b_ref[...])


def _rglru_layer(x, w_in, conv_w, conv_b, w_a, b_a, w_i, b_i, lam, w_out, j, g, b, i):
    s, d = x.shape
    _, n_blocks, blk, _ = w_a.shape
    n_taps = conv_w.shape[1]
    assert s % TM_LRU == 0 and n_taps - 1 < SUBLANES
    vec = _layer_block((1, d), j, 0, 0)
    return pl.pallas_call(
        _rglru_kernel,
        grid=(s // TM_LRU,),
        in_specs=[
            pl.BlockSpec((TM_LRU, d), lambda r: (r, 0)),
            _layer_block((d, d), j, 0, 0), _layer_block((d, d), j, 0, 1),
            _layer_block((n_taps, d), j, 0, 0), vec,
            _layer_block((n_blocks, blk, blk), j, 0, 0, 0), vec,
            _layer_block((n_blocks, blk, blk), j, 0, 0, 0), vec,
            vec,
            _layer_block((d, d), j, 0, 0),
            _layer_block((1, d), i, 0, 0), _layer_block((1, d), i, 0, 0),
        ],
        out_specs=pl.BlockSpec((TM_LRU, d), lambda i: (i, 0)),
        out_shape=jax.ShapeDtypeStruct((s, d), F32),
        scratch_shapes=[pltpu.VMEM((SUBLANES, d), F32), pltpu.VMEM((SUBLANES, d), F32)],
        compiler_params=pltpu.CompilerParams(
            dimension_semantics=("arbitrary",), vmem_limit_bytes=VMEM_LIMIT_BYTES),
        name="rglru",
    )(x, w_in, w_in, conv_w, conv_b, w_a, b_a, w_i, b_i, lam, w_out, g, b)


def _ffn_kernel(x_ref, wv_ref, wg_ref, cw_ref, cb_ref, wd_ref, g_ref, b_ref,
                p_ref, wp_ref, wpg_ref, bpg_ref, y_ref, prev_ref, h_ref, act_ref):
    tm, d = x_ref.shape
    d_ff = wd_ref.shape[0]
    tn = h_ref.shape[3]
    n_chunks = d_ff // tn
    n_taps = cw_ref.shape[0]

    @pl.when(pl.program_id(0) == 0)
    def _():
        prev_ref[...] = jnp.zeros_like(prev_ref)

    x = x_ref[...]
    xb = x.astype(BF16)

    def up(slot, j):
        cols = slice(j * tn, (j + 1) * tn)
        h_ref[slot, 0] = _dot(xb, wv_ref[:, cols])
        h_ref[slot, 1] = _dot(xb, wg_ref[:, cols])

    def conv(slot, part, j):
        cols = slice(part * d_ff + j * tn, part * d_ff + (j + 1) * tn)
        h = h_ref[slot, part]
        prev8 = prev_ref[:, cols]
        out = cb_ref[:, cols] + cw_ref[n_taps - 1:n_taps, cols] * h
        for k in range(1, n_taps):
            out = out + cw_ref[n_taps - 1 - k:n_taps - k, cols] * _shift_rows(prev8, h, k)
        prev_ref[:, cols] = h[tm - SUBLANES:, :]
        return out

    up(0, 0)
    for j in range(n_chunks):
        if j + 1 < n_chunks:
            up((j + 1) % 2, j + 1)
        val = conv(j % 2, 0, j)
        gate = conv(j % 2, 1, j)
        act_ref[:, j * tn:(j + 1) * tn] = (jax.nn.gelu(gate) * val).astype(BF16)

    ff = _dot(act_ref[...], wd_ref[...])
    x2 = _layer_norm(ALPHA * x + ff, g_ref[...], b_ref[...])
    gate = jax.nn.sigmoid(_dot(x2.astype(BF16), wpg_ref[...]) + bpg_ref[...])
    y_ref[...] = x2 + gate * _dot(p_ref[...].astype(BF16), wp_ref[...])


def _ffn_layer(x, w_up, conv_w, conv_b, w_down, g, b, p, ple_w, ple_gate_w, ple_gate_b, i, bi):
    s, d = x.shape
    d_ff = w_down.shape[1]
    d_ple = p.shape[3]
    n_taps = conv_w.shape[1]
    assert s % TM_FFN == 0 and d_ff % TN_FFN == 0 and n_taps - 1 < SUBLANES
    vec = _layer_block((1, d), i, 0, 0)
    return pl.pallas_call(
        _ffn_kernel,
        grid=(s // TM_FFN,),
        in_specs=[
            pl.BlockSpec((TM_FFN, d), lambda r: (r, 0)),
            _layer_block((d, d_ff), i, 0, 0), _layer_block((d, d_ff), i, 0, 1),
            _layer_block((n_taps, 2 * d_ff), i, 0, 0), _layer_block((1, 2 * d_ff), i, 0, 0),
            _layer_block((d_ff, d), i, 0, 0), vec, vec,
            pl.BlockSpec((None, None, TM_FFN, d_ple), lambda r: (i, bi, r, 0)),
            _layer_block((d_ple, d), i, 0, 0), _layer_block((d, d), i, 0, 0), vec,
        ],
        out_specs=pl.BlockSpec((TM_FFN, d), lambda i: (i, 0)),
        out_shape=jax.ShapeDtypeStruct((s, d), F32),
        scratch_shapes=[pltpu.VMEM((SUBLANES, 2 * d_ff), F32),
                        pltpu.VMEM((2, 2, TM_FFN, TN_FFN), F32),
                        pltpu.VMEM((TM_FFN, d_ff), BF16)],
        compiler_params=pltpu.CompilerParams(
            dimension_semantics=("arbitrary",), vmem_limit_bytes=VMEM_LIMIT_BYTES),
        name="conv_ffn",
    )(x, w_up, w_up, conv_w, conv_b, w_down, g, b, p, ple_w, ple_gate_w, ple_gate_b)


def kernel(x, p, a_w_in, a_b_f, a_w_out, b_w_in, b_conv_w, b_conv_b, b_w_a, b_b_a, b_w_i, b_b_i,
           b_lam, b_w_out, f_w_up, f_conv_w, f_conv_b, f_w_down, ln1_g, ln1_b, ln2_g, ln2_b,
           ple_w, ple_gate_w, ple_gate_b):
    bsz, s, d = x.shape
    depth = p.shape[0]
    n_heads = a_b_f.shape[1]
    bf16 = lambda w: w.astype(BF16)
    attn_w = _attn_weights(a_w_in, a_b_f)
    a_w_out, b_w_in, b_w_a, b_w_i, b_w_out = map(bf16, (a_w_out, b_w_in, b_w_a, b_w_i, b_w_out))
    f_w_up, f_w_down, ple_w, ple_gate_w = map(bf16, (f_w_up, f_w_down, ple_w, ple_gate_w))
    b_b_a, b_b_i = (v.reshape(v.shape[0], 1, d) for v in (b_b_a, b_b_i))
    b_conv_b, b_lam, f_conv_b, ple_gate_b = map(_rows, (b_conv_b, b_lam, f_conv_b, ple_gate_b))
    ln1_g, ln1_b, ln2_g, ln2_b = map(_rows, (ln1_g, ln1_b, ln2_g, ln2_b))
    outs = []
    for bi in range(bsz):
        xs = x[bi]
        for i in range(depth):
            j = i // 2
            if i % 2 == 0:
                q_aug, k_aug, vt = _attn_proj(xs, attn_w, j, n_heads)
                o = _flash(q_aug, k_aug, vt)
                xs = _attn_out(o, a_w_out, j, xs, ln1_g, ln1_b, i)
            else:
                xs = _rglru_layer(xs, b_w_in, b_conv_w, b_conv_b, b_w_a, b_b_a, b_w_i, b_b_i,
                                  b_lam, b_w_out, j, ln1_g, ln1_b, i)
            xs = _ffn_layer(xs, f_w_up, f_conv_w, f_conv_b, f_w_down, ln2_g, ln2_b, p,
                            ple_w, ple_gate_w, ple_gate_b, i, bi)
        outs.append(xs)
    return jnp.stack(outs, axis=0)
```

```python
import functools
import math

import jax
import jax.numpy as jnp
from jax import lax
from jax.experimental import pallas as pl
from jax.experimental.pallas import tpu as pltpu

N_HEADS = 16
LRU_C = 8.0
LN_EPS = 1e-5
DEPTH = 4
ALPHA = (2.0 * DEPTH) ** 0.25
LOG2E = math.log2(math.e)

LANES = 128
SUBLANES = 8
VMEM_LIMIT_BYTES = 56 * 1024 * 1024

TQ = 2048
TK = 256
QSTRIP = 256
TM_PROJ = 512
TM_FFN = 512
TN_FFN = 256
TM_LRU = 256

AUG_WIDTH = 6
VT_ROWS = LANES // 2 + 2 * SUBLANES
NEG_BIG = -1e30

BF16 = jnp.bfloat16
F32 = jnp.float32
NT_DIMS = (((1,), (1,)), ((), ()))


def _resident(shape):
    n = len(shape)
    return pl.BlockSpec(shape, lambda *_: (0,) * n, pipeline_mode=pl.Buffered(1))


def _layer_block(block, *index):
    return pl.BlockSpec((None,) + tuple(block), lambda *_: tuple(index),
                        pipeline_mode=pl.Buffered(1))


def _rows(v):
    return v.reshape(v.shape[0], 1, -1)


def _dot(a, b):
    return jnp.dot(a, b, preferred_element_type=F32)


def _layer_norm(y, g, b):
    mu = jnp.mean(y, axis=-1, keepdims=True)
    yc = y - mu
    var = jnp.mean(yc * yc, axis=-1, keepdims=True)
    return yc * lax.rsqrt(var + LN_EPS) * g + b


def _sigmoid(x):
    return 0.5 * jnp.tanh(0.5 * x) + 0.5


def _split3(x):
    p1 = x.astype(BF16)
    r1 = x - p1.astype(F32)
    p2 = r1.astype(BF16)
    p3 = (r1 - p2.astype(F32)).astype(BF16)
    return p1, p2, p3


def _shift_rows(prev8, cur, k):
    both = jnp.concatenate([prev8, cur], axis=0)
    return pltpu.roll(both, k, 0)[SUBLANES:]


def _attn_proj_kernel(x_ref, wq_ref, wk_ref, wvt_ref, wf_ref, bf_ref, tri_ref, pq_ref, pk_ref,
                      q_ref, k_ref, vt_ref, carry_ref, *, n_heads, scale):
    tm = x_ref.shape[0]

    @pl.when(pl.program_id(0) == 0)
    def _():
        carry_ref[...] = jnp.zeros_like(carry_ref)

    xb = x_ref[...].astype(BF16)
    lane = lax.broadcasted_iota(jnp.int32, (tm, LANES), 1)

    fg = _dot(xb, wf_ref[...]) + bf_ref[...]
    lf = (jnp.minimum(fg, 0.0) - jnp.log1p(jnp.exp(-jnp.abs(fg)))) * LOG2E
    lf = jnp.where(lane < n_heads, lf, 0.0)

    cs = _dot(tri_ref[...], jnp.concatenate(_split3(lf), axis=1))
    c = cs[:, :LANES] + cs[:, LANES:2 * LANES] + cs[:, 2 * LANES:] + carry_ref[0:1, :]
    carry_ref[...] = jnp.broadcast_to(c[tm - 1:tm, :], carry_ref.shape)

    c1, c2, c3 = _split3(c)
    cz = (c1.astype(F32) + pltpu.roll(c2.astype(F32), n_heads, 1)
          + pltpu.roll(c3.astype(F32), 2 * n_heads, 1))
    cz = jnp.where(lane == 3 * n_heads, 1.0, cz).astype(BF16)
    eq = _dot(cz, pq_ref[...])
    ek = _dot(cz, pk_ref[...])

    qf = _dot(xb, wq_ref[...]) * (scale * LOG2E)
    kf = _dot(xb, wk_ref[...])
    half = LANES // 2
    for pair in range(n_heads // 2):
        cols = slice(pair * LANES, (pair + 1) * LANES)
        q_ref[2 * pair] = jnp.where(lane < half, qf[:, cols], eq[:, cols]).astype(BF16)
        q_ref[2 * pair + 1] = jnp.where(lane >= half, qf[:, cols], eq[:, cols]).astype(BF16)
        k_ref[2 * pair] = jnp.where(lane < half, kf[:, cols], ek[:, cols]).astype(BF16)
        k_ref[2 * pair + 1] = jnp.where(lane >= half, kf[:, cols], ek[:, cols]).astype(BF16)

    vt = lax.dot_general(wvt_ref[...], xb, NT_DIMS, preferred_element_type=F32).astype(BF16)
    dh = LANES // 2
    tk = vt_ref.shape[3]
    ones_rows = jnp.where(lax.broadcasted_iota(jnp.int32, (VT_ROWS - dh, tk), 0) == 0, 1.0, 0.0)
    for blk in range(tm // tk):
        for h in range(n_heads):
            vt_ref[blk, h, :dh, :] = vt[h * dh:(h + 1) * dh, blk * tk:(blk + 1) * tk]
            vt_ref[blk, h, dh:, :] = ones_rows.astype(BF16)


def _placement_matrices(n_heads):
    import numpy as np
    pq = np.zeros((LANES, n_heads // 2 * LANES), np.float32)
    pk = np.zeros((LANES, n_heads // 2 * LANES), np.float32)
    one_lane = 3 * n_heads
    for h in range(n_heads):
        base = (h // 2) * LANES + (LANES // 2 if h % 2 == 0 else 0)
        for piece in range(3):
            pq[piece * n_heads + h, base + piece] = 1.0
            pq[one_lane, base + 3 + piece] = 1.0
            pk[one_lane, base + piece] = 1.0
            pk[piece * n_heads + h, base + 3 + piece] = -1.0
    return jnp.asarray(pq, BF16), jnp.asarray(pk, BF16)


def _attn_weights(a_w_in, a_b_f):
    n_a, d, _ = a_w_in.shape
    n_heads = a_b_f.shape[1]
    wvt = jnp.swapaxes(a_w_in[:, :, 2 * d:3 * d], 1, 2).astype(BF16)
    wf = jnp.pad(a_w_in[:, :, 3 * d:], ((0, 0), (0, 0), (0, LANES - n_heads))).astype(BF16)
    bf = jnp.pad(a_b_f, ((0, 0), (0, LANES - n_heads))).reshape(n_a, 1, LANES)
    return a_w_in.astype(BF16), wvt, wf, bf


def _attn_proj(x, weights, j, n_heads):
    w_in, wvt, wf, bf = weights
    s, d = x.shape
    dh = d // n_heads
    assert 2 * dh == LANES and 3 * n_heads < LANES and s % TM_PROJ == 0
    tri = jnp.tril(jnp.ones((TM_PROJ, TM_PROJ), BF16))
    pq, pk = _placement_matrices(n_heads)
    kern = functools.partial(_attn_proj_kernel, n_heads=n_heads, scale=1.0 / math.sqrt(dh))
    return pl.pallas_call(
        kern,
        grid=(s // TM_PROJ,),
        in_specs=[
            pl.BlockSpec((TM_PROJ, d), lambda i: (i, 0)),
            _layer_block((d, d), j, 0, 0), _layer_block((d, d), j, 0, 1),
            _layer_block((d, d), j, 0, 0),
            _layer_block((d, LANES), j, 0, 0), _layer_block((1, LANES), j, 0, 0),
            _resident((TM_PROJ, TM_PROJ)),
            _resident(pq.shape), _resident(pk.shape),
        ],
        out_specs=[
            pl.BlockSpec((n_heads, TM_PROJ, LANES), lambda i: (0, i, 0)),
            pl.BlockSpec((n_heads, TM_PROJ, LANES), lambda i: (0, i, 0)),
            pl.BlockSpec((TM_PROJ // TK, n_heads, VT_ROWS, TK), lambda i: (i, 0, 0, 0)),
        ],
        out_shape=[
            jax.ShapeDtypeStruct((n_heads, s, LANES), BF16),
            jax.ShapeDtypeStruct((n_heads, s, LANES), BF16),
            jax.ShapeDtypeStruct((s // TK, n_heads, VT_ROWS, TK), BF16),
        ],
        scratch_shapes=[pltpu.VMEM((SUBLANES, LANES), F32)],
        compiler_params=pltpu.CompilerParams(
            dimension_semantics=("arbitrary",), vmem_limit_bytes=VMEM_LIMIT_BYTES),
        name="attn_proj",
    )(x, w_in, w_in, wvt, wf, bf, tri, pq, pk)


def _flash_kernel(q_ref, k_ref, vt_ref, o_ref, s_ref, acc_ref):
    tq = q_ref.shape[1]
    tk = vt_ref.shape[3]
    dh = LANES // 2
    qi = pl.program_id(1)
    qs = (q_ref[0], q_ref[1])

    tri = (lax.broadcasted_iota(jnp.int32, (tk, QSTRIP), 0)
           <= lax.broadcasted_iota(jnp.int32, (tk, QSTRIP), 1))

    def scores(slot, j, c0=0, masked=False):
        maxima = []
        for hh in range(2):
            k = k_ref[hh, pl.ds(pl.multiple_of(j * tk, tk), tk), :]
            s = lax.dot_general(k, qs[hh][c0:], NT_DIMS, preferred_element_type=F32)
            if masked:
                parts = [jnp.where(tri, s[:, :QSTRIP], NEG_BIG)]
                if tq - c0 > QSTRIP:
                    parts.append(s[:, QSTRIP:])
                s = jnp.concatenate(parts, axis=1)
            s_ref[slot, hh, :, c0:] = s
            maxima.append(jnp.max(s, axis=0, keepdims=True))
        return tuple(maxima)

    def update(slot, j, ms, maxima, c0=0, masked=False):
        out = []
        for hh in range(2):
            m = ms[hh]
            new = [m[:, :c0]] if c0 else []
            for q0 in range(c0, tq, QSTRIP):
                cols = slice(q0, q0 + QSTRIP)
                s = s_ref[slot, hh, :, cols]
                if masked and q0 == c0:
                    s = jnp.where(tri, s, NEG_BIG)
                    s_max = jnp.max(s, axis=0, keepdims=True)
                else:
                    s_max = maxima[hh][:, q0 - c0:q0 - c0 + QSTRIP]
                m_new = jnp.maximum(m[:, cols], s_max)
                p = jnp.exp2(s - m_new).astype(BF16)
                acc_ref[hh, :, cols] = (jnp.exp2(m[:, cols] - m_new) * acc_ref[hh, :, cols]
                                        + _dot(vt_ref[j, hh], p))
                new.append(m_new)
            out.append(jnp.concatenate(new, axis=1))
        return tuple(out)

    n_sub = tq // tk

    def body(jj, carry):
        ms, maxima = carry
        for b in range(n_sub):
            next_maxima = scores((b + 1) % 2, n_sub * jj + b + 1)
            ms = update(b % 2, n_sub * jj + b, ms, maxima)
            maxima = next_maxima
        return ms, maxima

    acc_ref[...] = jnp.zeros(acc_ref.shape, F32)
    m0 = jnp.full((1, tq), NEG_BIG, F32)
    ms, maxima = lax.fori_loop(0, qi, body, ((m0, m0), scores(0, 0)))
    for b in range(n_sub):
        if b + 1 < n_sub:
            next_maxima = scores((b + 1) % 2, n_sub * qi + b + 1, (b + 1) * tk, masked=True)
        ms = update(b % 2, n_sub * qi + b, ms, maxima, b * tk, masked=(b == 0))
        maxima = next_maxima
    outs = [acc_ref[hh, :dh, :] / acc_ref[hh, dh:dh + 1, :] for hh in range(2)]
    o_ref[...] = jnp.concatenate(outs, axis=0).T.astype(o_ref.dtype)


def _flash(q_aug, k_aug, vt):
    n_heads, s, _ = q_aug.shape
    n_kv, _, vt_rows, tk = vt.shape
    d = n_heads * (LANES // 2)
    assert TQ % (2 * tk) == 0 and s % TQ == 0 and tk == QSTRIP
    return pl.pallas_call(
        _flash_kernel,
        grid=(n_heads // 2, s // TQ),
        in_specs=[
            pl.BlockSpec((2, TQ, LANES), lambda hp, qi: (hp, qi, 0)),
            pl.BlockSpec((2, s, LANES), lambda hp, qi: (hp, 0, 0)),
            pl.BlockSpec((n_kv, 2, vt_rows, tk), lambda hp, qi: (0, hp, 0, 0)),
        ],
        out_specs=pl.BlockSpec((TQ, LANES), lambda hp, qi: (qi, hp)),
        out_shape=jax.ShapeDtypeStruct((s, d), BF16),
        scratch_shapes=[pltpu.VMEM((2, 2, tk, TQ), F32), pltpu.VMEM((2, vt_rows, TQ), F32)],
        compiler_params=pltpu.CompilerParams(
            dimension_semantics=("arbitrary", "arbitrary"), vmem_limit_bytes=VMEM_LIMIT_BYTES),
        name="flash_attn",
    )(q_aug, k_aug, vt)


def _rglru_kernel(x_ref, wx_ref, wg_ref, cw_ref, cb_ref, wa_ref, ba_ref, wi_ref, bi_ref,
                  lam_ref, wo_ref, g_ref, b_ref, y_ref, xprev_ref, hprev_ref):
    tm, d = x_ref.shape
    n_blocks = wa_ref.shape[0]
    blk = d // n_blocks
    n_taps = cw_ref.shape[0]

    @pl.when(pl.program_id(0) == 0)
    def _():
        xprev_ref[...] = jnp.zeros_like(xprev_ref)
        hprev_ref[...] = jnp.zeros_like(hprev_ref)

    x = x_ref[...]
    xb = x.astype(BF16)
    xp = _dot(xb, wx_ref[...])
    gb = _dot(xb, wg_ref[...])

    prev8 = xprev_ref[...]
    xc = cb_ref[...] + cw_ref[n_taps - 1:n_taps, :] * xp
    for k in range(1, n_taps):
        xc = xc + cw_ref[n_taps - 1 - k:n_taps - k, :] * _shift_rows(prev8, xp, k)
    xprev_ref[...] = xp[tm - SUBLANES:, :]

    r_parts, i_parts = [], []
    for n in range(n_blocks):
        xh = xc[:, n * blk:(n + 1) * blk].astype(BF16)
        r_parts.append(_dot(xh, wa_ref[n]))
        i_parts.append(_dot(xh, wi_ref[n]))
    r = _sigmoid(jnp.concatenate(r_parts, axis=1) + ba_ref[...])
    ig = _sigmoid(jnp.concatenate(i_parts, axis=1) + bi_ref[...])

    log_a = (-LRU_C) * r * jax.nn.softplus(-lam_ref[...])
    a = jnp.exp(log_a)
    u = jnp.sqrt(-jnp.tanh(log_a) * (a * a + 1.0)) * (ig * xc)

    n_groups = tm // SUBLANES
    a = a.reshape(n_groups, SUBLANES, d)
    u = u.reshape(n_groups, SUBLANES, d)
    sub = lax.broadcasted_iota(jnp.int32, (1, SUBLANES, d), 1)
    dist = 1
    while dist < SUBLANES:
        a_sh = jnp.where(sub >= dist, pltpu.roll(a, dist, 1), 1.0)
        u_sh = jnp.where(sub >= dist, pltpu.roll(u, dist, 1), 0.0)
        u = u + a * u_sh
        a = a * a_sh
        dist *= 2
    h_last = hprev_ref[0:1, :]
    groups = []
    for g in range(n_groups):
        h_g = u[g] + a[g] * h_last
        groups.append(h_g)
        h_last = h_g[SUBLANES - 1:SUBLANES, :]
    h = jnp.concatenate(groups, axis=0)
    hprev_ref[...] = jnp.broadcast_to(h_last, hprev_ref.shape)

    yv = (h * jax.nn.gelu(gb)).astype(BF16)
    m = _dot(yv, wo_ref[...])
    y_ref[...] = _layer_norm(ALPHA * x + m, g_ref[...], b_ref[...])


def _rglru_layer(x, w_in, conv_w, conv_b, w_a, b_a, w_i, b_i, lam, w_out, j, g, b, i):
    s, d = x.shape
    _, n_blocks, blk, _ = w_a.shape
    n_taps = conv_w.shape[1]
    assert s % TM_LRU == 0 and n_taps - 1 < SUBLANES
    vec = _layer_block((1, d), j, 0, 0)
    return pl.pallas_call(
        _rglru_kernel,
        grid=(s // TM_LRU,),
        in_specs=[
            pl.BlockSpec((TM_LRU, d), lambda r: (r, 0)),
            _layer_block((d, d), j, 0, 0), _layer_block((d, d), j, 0, 1),
            _layer_block((n_taps, d), j, 0, 0), vec,
            _layer_block((n_blocks, blk, blk), j, 0, 0, 0), vec,
            _layer_block((n_blocks, blk, blk), j, 0, 0, 0), vec,
            vec,
            _layer_block((d, d), j, 0, 0),
            _layer_block((1, d), i, 0, 0), _layer_block((1, d), i, 0, 0),
        ],
        out_specs=pl.BlockSpec((TM_LRU, d), lambda i: (i, 0)),
        out_shape=jax.ShapeDtypeStruct((s, d), F32),
        scratch_shapes=[pltpu.VMEM((SUBLANES, d), F32), pltpu.VMEM((SUBLANES, d), F32)],
        compiler_params=pltpu.CompilerParams(
            dimension_semantics=("arbitrary",), vmem_limit_bytes=VMEM_LIMIT_BYTES),
        name="rglru",
    )(x, w_in, w_in, conv_w, conv_b, w_a, b_a, w_i, b_i, lam, w_out, g, b)


def _ffn_kernel(*refs, attn_out):
    if attn_out:
        o_ref, wo_ref, g1_ref, b1_ref, *refs = refs
    (x_ref, wv_ref, wg_ref, cw_ref, cb_ref, wd_ref, g_ref, b_ref,
     p_ref, wp_ref, wpg_ref, bpg_ref, y_ref, prev_ref, h_ref, act_ref) = refs
    tm, d = x_ref.shape
    d_ff = wd_ref.shape[0]
    tn = h_ref.shape[3]
    n_chunks = d_ff // tn
    n_taps = cw_ref.shape[0]

    @pl.when(pl.program_id(0) == 0)
    def _():
        prev_ref[...] = jnp.zeros_like(prev_ref)

    x = x_ref[...]
    if attn_out:
        x = _layer_norm(ALPHA * x + _dot(o_ref[...], wo_ref[...]), g1_ref[...], b1_ref[...])
    xb = x.astype(BF16)

    def up(slot, j):
        cols = slice(j * tn, (j + 1) * tn)
        h_ref[slot, 0] = _dot(xb, wv_ref[:, cols])
        h_ref[slot, 1] = _dot(xb, wg_ref[:, cols])

    def conv(slot, part, j):
        cols = slice(part * d_ff + j * tn, part * d_ff + (j + 1) * tn)
        h = h_ref[slot, part]
        prev8 = prev_ref[:, cols]
        out = cb_ref[:, cols] + cw_ref[n_taps - 1:n_taps, cols] * h
        for k in range(1, n_taps):
            out = out + cw_ref[n_taps - 1 - k:n_taps - k, cols] * _shift_rows(prev8, h, k)
        prev_ref[:, cols] = h[tm - SUBLANES:, :]
        return out

    up(0, 0)
    for j in range(n_chunks):
        if j + 1 < n_chunks:
            up((j + 1) % 2, j + 1)
        val = conv(j % 2, 0, j)
        gate = conv(j % 2, 1, j)
        act_ref[:, j * tn:(j + 1) * tn] = (jax.nn.gelu(gate) * val).astype(BF16)

    ff = _dot(act_ref[...], wd_ref[...])
    x2 = _layer_norm(ALPHA * x + ff, g_ref[...], b_ref[...])
    gate = _sigmoid(_dot(x2.astype(BF16), wpg_ref[...]) + bpg_ref[...])
    y_ref[...] = x2 + gate * _dot(p_ref[...].astype(BF16), wp_ref[...])


def _ffn_layer(x, w_up, conv_w, conv_b, w_down, g, b, p, ple_w, ple_gate_w, ple_gate_b, i, bi,
               attn=None):
    s, d = x.shape
    d_ff = w_down.shape[1]
    d_ple = p.shape[3]
    n_taps = conv_w.shape[1]
    assert s % TM_FFN == 0 and d_ff % TN_FFN == 0 and n_taps - 1 < SUBLANES
    vec = _layer_block((1, d), i, 0, 0)
    attn_specs, attn_args = [], ()
    if attn is not None:
        o, w_out, j, g1, b1 = attn
        attn_specs = [pl.BlockSpec((TM_FFN, d), lambda r: (r, 0)), _layer_block((d, d), j, 0, 0),
                      vec, vec]
        attn_args = (o, w_out, g1, b1)
    return pl.pallas_call(
        functools.partial(_ffn_kernel, attn_out=attn is not None),
        grid=(s // TM_FFN,),
        in_specs=attn_specs + [
            pl.BlockSpec((TM_FFN, d), lambda r: (r, 0)),
            _layer_block((d, d_ff), i, 0, 0), _layer_block((d, d_ff), i, 0, 1),
            _layer_block((n_taps, 2 * d_ff), i, 0, 0), _layer_block((1, 2 * d_ff), i, 0, 0),
            _layer_block((d_ff, d), i, 0, 0), vec, vec,
            pl.BlockSpec((None, None, TM_FFN, d_ple), lambda r: (i, bi, r, 0)),
            _layer_block((d_ple, d), i, 0, 0), _layer_block((d, d), i, 0, 0), vec,
        ],
        out_specs=pl.BlockSpec((TM_FFN, d), lambda i: (i, 0)),
        out_shape=jax.ShapeDtypeStruct((s, d), F32),
        scratch_shapes=[pltpu.VMEM((SUBLANES, 2 * d_ff), F32),
                        pltpu.VMEM((2, 2, TM_FFN, TN_FFN), F32),
                        pltpu.VMEM((TM_FFN, d_ff), BF16)],
        compiler_params=pltpu.CompilerParams(
            dimension_semantics=("arbitrary",), vmem_limit_bytes=VMEM_LIMIT_BYTES),
        name="conv_ffn",
    )(*attn_args, x, w_up, w_up, conv_w, conv_b, w_down, g, b, p, ple_w, ple_gate_w, ple_gate_b)


def kernel(x, p, a_w_in, a_b_f, a_w_out, b_w_in, b_conv_w, b_conv_b, b_w_a, b_b_a, b_w_i, b_b_i,
           b_lam, b_w_out, f_w_up, f_conv_w, f_conv_b, f_w_down, ln1_g, ln1_b, ln2_g, ln2_b,
           ple_w, ple_gate_w, ple_gate_b):
    bsz, s, d = x.shape
    depth = p.shape[0]
    n_heads = a_b_f.shape[1]
    bf16 = lambda w: w.astype(BF16)
    attn_w = _attn_weights(a_w_in, a_b_f)
    a_w_out, b_w_in, b_w_a, b_w_i, b_w_out = map(bf16, (a_w_out, b_w_in, b_w_a, b_w_i, b_w_out))
    f_w_up, f_w_down, ple_w, ple_gate_w = map(bf16, (f_w_up, f_w_down, ple_w, ple_gate_w))
    b_b_a, b_b_i = (v.reshape(v.shape[0], 1, d) for v in (b_b_a, b_b_i))
    b_conv_b, b_lam, f_conv_b, ple_gate_b = map(_rows, (b_conv_b, b_lam, f_conv_b, ple_gate_b))
    ln1_g, ln1_b, ln2_g, ln2_b = map(_rows, (ln1_g, ln1_b, ln2_g, ln2_b))
    outs = []
    for bi in range(bsz):
        xs = x[bi]
        for i in range(depth):
            j = i // 2
            attn = None
            if i % 2 == 0:
                q_aug, k_aug, vt = _attn_proj(xs, attn_w, j, n_heads)
                attn = (_flash(q_aug, k_aug, vt), a_w_out, j, ln1_g, ln1_b)
            else:
                xs = _rglru_layer(xs, b_w_in, b_conv_w, b_conv_b, b_w_a, b_b_a, b_w_i, b_b_i,
                                  b_lam, b_w_out, j, ln1_g, ln1_b, i)
            xs = _ffn_layer(xs, f_w_up, f_conv_w, f_conv_b, f_w_down, ln2_g, ln2_b, p,
                            ple_w, ple_gate_w, ple_gate_b, i, bi, attn)
        outs.append(xs)
    return jnp.stack(outs, axis=0)
```

```python
import functools
import math

import jax
import jax.numpy as jnp
from jax import lax
from jax.experimental import pallas as pl
from jax.experimental.pallas import tpu as pltpu

N_HEADS = 16
LRU_C = 8.0
LN_EPS = 1e-5
DEPTH = 4
ALPHA = (2.0 * DEPTH) ** 0.25
LOG2E = math.log2(math.e)

LANES = 128
SUBLANES = 8
VMEM_LIMIT_BYTES = 56 * 1024 * 1024

TQ = 2048
TK = 512
QSTRIP = 512
TM_PROJ = 512
TM_FFN = 512
TN_FFN = 256
TM_LRU = 256

AUG_WIDTH = 6
VT_ROWS = LANES // 2 + 2 * SUBLANES
NEG_BIG = -1e30

BF16 = jnp.bfloat16
F32 = jnp.float32
NT_DIMS = (((1,), (1,)), ((), ()))


def _resident(shape):
    n = len(shape)
    return pl.BlockSpec(shape, lambda *_: (0,) * n, pipeline_mode=pl.Buffered(1))


def _layer_block(block, *index):
    return pl.BlockSpec((None,) + tuple(block), lambda *_: tuple(index),
                        pipeline_mode=pl.Buffered(1))


def _rows(v):
    return v.reshape(v.shape[0], 1, -1)


def _dot(a, b):
    return jnp.dot(a, b, preferred_element_type=F32)


def _layer_norm(y, g, b):
    mu = jnp.mean(y, axis=-1, keepdims=True)
    yc = y - mu
    var = jnp.mean(yc * yc, axis=-1, keepdims=True)
    return yc * lax.rsqrt(var + LN_EPS) * g + b


def _sigmoid(x):
    return 0.5 * jnp.tanh(0.5 * x) + 0.5


def _split3(x):
    p1 = x.astype(BF16)
    r1 = x - p1.astype(F32)
    p2 = r1.astype(BF16)
    p3 = (r1 - p2.astype(F32)).astype(BF16)
    return p1, p2, p3


def _shift_rows(prev8, cur, k):
    both = jnp.concatenate([prev8, cur], axis=0)
    return pltpu.roll(both, k, 0)[SUBLANES:]


def _attn_proj_kernel(x_ref, wq_ref, wk_ref, wvt_ref, wf_ref, bf_ref, tri_ref, pq_ref, pk_ref,
                      q_ref, k_ref, vt_ref, carry_ref, *, n_heads, scale):
    tm = x_ref.shape[0]

    @pl.when(pl.program_id(0) == 0)
    def _():
        carry_ref[...] = jnp.zeros_like(carry_ref)

    xb = x_ref[...].astype(BF16)
    lane = lax.broadcasted_iota(jnp.int32, (tm, LANES), 1)

    fg = _dot(xb, wf_ref[...]) + bf_ref[...]
    lf = (jnp.minimum(fg, 0.0) - jnp.log1p(jnp.exp(-jnp.abs(fg)))) * LOG2E
    lf = jnp.where(lane < n_heads, lf, 0.0)

    cs = _dot(tri_ref[...], jnp.concatenate(_split3(lf), axis=1))
    c = cs[:, :LANES] + cs[:, LANES:2 * LANES] + cs[:, 2 * LANES:] + carry_ref[0:1, :]
    carry_ref[...] = jnp.broadcast_to(c[tm - 1:tm, :], carry_ref.shape)

    c1, c2, c3 = _split3(c)
    cz = (c1.astype(F32) + pltpu.roll(c2.astype(F32), n_heads, 1)
          + pltpu.roll(c3.astype(F32), 2 * n_heads, 1))
    cz = jnp.where(lane == 3 * n_heads, 1.0, cz).astype(BF16)
    eq = _dot(cz, pq_ref[...])
    ek = _dot(cz, pk_ref[...])

    qf = _dot(xb, wq_ref[...]) * (scale * LOG2E)
    kf = _dot(xb, wk_ref[...])
    half = LANES // 2
    for pair in range(n_heads // 2):
        cols = slice(pair * LANES, (pair + 1) * LANES)
        q_ref[2 * pair] = jnp.where(lane < half, qf[:, cols], eq[:, cols]).astype(BF16)
        q_ref[2 * pair + 1] = jnp.where(lane >= half, qf[:, cols], eq[:, cols]).astype(BF16)
        k_ref[2 * pair] = jnp.where(lane < half, kf[:, cols], ek[:, cols]).astype(BF16)
        k_ref[2 * pair + 1] = jnp.where(lane >= half, kf[:, cols], ek[:, cols]).astype(BF16)

    vt = lax.dot_general(wvt_ref[...], xb, NT_DIMS, preferred_element_type=F32).astype(BF16)
    dh = LANES // 2
    tk = vt_ref.shape[3]
    ones_rows = jnp.where(lax.broadcasted_iota(jnp.int32, (VT_ROWS - dh, tk), 0) == 0, 1.0, 0.0)
    for blk in range(tm // tk):
        for h in range(n_heads):
            vt_ref[blk, h, :dh, :] = vt[h * dh:(h + 1) * dh, blk * tk:(blk + 1) * tk]
            vt_ref[blk, h, dh:, :] = ones_rows.astype(BF16)


def _placement_matrices(n_heads):
    import numpy as np
    pq = np.zeros((LANES, n_heads // 2 * LANES), np.float32)
    pk = np.zeros((LANES, n_heads // 2 * LANES), np.float32)
    one_lane = 3 * n_heads
    for h in range(n_heads):
        base = (h // 2) * LANES + (LANES // 2 if h % 2 == 0 else 0)
        for piece in range(3):
            pq[piece * n_heads + h, base + piece] = 1.0
            pq[one_lane, base + 3 + piece] = 1.0
            pk[one_lane, base + piece] = 1.0
            pk[piece * n_heads + h, base + 3 + piece] = -1.0
    return jnp.asarray(pq, BF16), jnp.asarray(pk, BF16)


def _attn_weights(a_w_in, a_b_f):
    n_a, d, _ = a_w_in.shape
    n_heads = a_b_f.shape[1]
    wvt = jnp.swapaxes(a_w_in[:, :, 2 * d:3 * d], 1, 2).astype(BF16)
    wf = jnp.pad(a_w_in[:, :, 3 * d:], ((0, 0), (0, 0), (0, LANES - n_heads))).astype(BF16)
    bf = jnp.pad(a_b_f, ((0, 0), (0, LANES - n_heads))).reshape(n_a, 1, LANES)
    return a_w_in.astype(BF16), wvt, wf, bf


def _attn_proj(x, weights, j, n_heads):
    w_in, wvt, wf, bf = weights
    s, d = x.shape
    dh = d // n_heads
    assert 2 * dh == LANES and 3 * n_heads < LANES and s % TM_PROJ == 0
    tri = jnp.tril(jnp.ones((TM_PROJ, TM_PROJ), BF16))
    pq, pk = _placement_matrices(n_heads)
    kern = functools.partial(_attn_proj_kernel, n_heads=n_heads, scale=1.0 / math.sqrt(dh))
    return pl.pallas_call(
        kern,
        grid=(s // TM_PROJ,),
        in_specs=[
            pl.BlockSpec((TM_PROJ, d), lambda i: (i, 0)),
            _layer_block((d, d), j, 0, 0), _layer_block((d, d), j, 0, 1),
            _layer_block((d, d), j, 0, 0),
            _layer_block((d, LANES), j, 0, 0), _layer_block((1, LANES), j, 0, 0),
            _resident((TM_PROJ, TM_PROJ)),
            _resident(pq.shape), _resident(pk.shape),
        ],
        out_specs=[
            pl.BlockSpec((n_heads, TM_PROJ, LANES), lambda i: (0, i, 0)),
            pl.BlockSpec((n_heads, TM_PROJ, LANES), lambda i: (0, i, 0)),
            pl.BlockSpec((TM_PROJ // TK, n_heads, VT_ROWS, TK), lambda i: (i, 0, 0, 0)),
        ],
        out_shape=[
            jax.ShapeDtypeStruct((n_heads, s, LANES), BF16),
            jax.ShapeDtypeStruct((n_heads, s, LANES), BF16),
            jax.ShapeDtypeStruct((s // TK, n_heads, VT_ROWS, TK), BF16),
        ],
        scratch_shapes=[pltpu.VMEM((SUBLANES, LANES), F32)],
        compiler_params=pltpu.CompilerParams(
            dimension_semantics=("arbitrary",), vmem_limit_bytes=VMEM_LIMIT_BYTES),
        name="attn_proj",
    )(x, w_in, w_in, wvt, wf, bf, tri, pq, pk)


def _flash_kernel(q_ref, k_ref, vt_ref, o_ref, s_ref, acc_ref):
    tq = q_ref.shape[1]
    tk = vt_ref.shape[3]
    dh = LANES // 2
    qi = pl.program_id(1)
    qs = (q_ref[0], q_ref[1])

    tri = (lax.broadcasted_iota(jnp.int32, (tk, QSTRIP), 0)
           <= lax.broadcasted_iota(jnp.int32, (tk, QSTRIP), 1))

    def scores(slot, j, c0=0, masked=False):
        maxima = []
        for hh in range(2):
            k = k_ref[hh, pl.ds(pl.multiple_of(j * tk, tk), tk), :]
            s = lax.dot_general(k, qs[hh][c0:], NT_DIMS, preferred_element_type=F32)
            if masked:
                parts = [jnp.where(tri, s[:, :QSTRIP], NEG_BIG)]
                if tq - c0 > QSTRIP:
                    parts.append(s[:, QSTRIP:])
                s = jnp.concatenate(parts, axis=1)
            s_ref[slot, hh, :, c0:] = s
            maxima.append(jnp.max(s, axis=0, keepdims=True))
        return tuple(maxima)

    def update(slot, j, ms, maxima, c0=0, masked=False):
        out = []
        for hh in range(2):
            m = ms[hh]
            new = [m[:, :c0]] if c0 else []
            for q0 in range(c0, tq, QSTRIP):
                cols = slice(q0, q0 + QSTRIP)
                s = s_ref[slot, hh, :, cols]
                if masked and q0 == c0:
                    s = jnp.where(tri, s, NEG_BIG)
                    s_max = jnp.max(s, axis=0, keepdims=True)
                else:
                    s_max = maxima[hh][:, q0 - c0:q0 - c0 + QSTRIP]
                m_new = jnp.maximum(m[:, cols], s_max)
                p = jnp.exp2(s - m_new).astype(BF16)
                acc_ref[hh, :, cols] = (jnp.exp2(m[:, cols] - m_new) * acc_ref[hh, :, cols]
                                        + _dot(vt_ref[j, hh], p))
                new.append(m_new)
            out.append(jnp.concatenate(new, axis=1))
        return tuple(out)

    n_sub = tq // tk

    def body(jj, carry):
        ms, maxima = carry
        for b in range(n_sub):
            next_maxima = scores((b + 1) % 2, n_sub * jj + b + 1)
            ms = update(b % 2, n_sub * jj + b, ms, maxima)
            maxima = next_maxima
        return ms, maxima

    acc_ref[...] = jnp.zeros(acc_ref.shape, F32)
    m0 = jnp.full((1, tq), NEG_BIG, F32)
    ms, maxima = lax.fori_loop(0, qi, body, ((m0, m0), scores(0, 0)))
    for b in range(n_sub):
        if b + 1 < n_sub:
            next_maxima = scores((b + 1) % 2, n_sub * qi + b + 1, (b + 1) * tk, masked=True)
        ms = update(b % 2, n_sub * qi + b, ms, maxima, b * tk, masked=(b == 0))
        maxima = next_maxima
    outs = [acc_ref[hh, :dh, :] / acc_ref[hh, dh:dh + 1, :] for hh in range(2)]
    o_ref[...] = jnp.concatenate(outs, axis=0).T.astype(o_ref.dtype)


def _flash(q_aug, k_aug, vt):
    n_heads, s, _ = q_aug.shape
    n_kv, _, vt_rows, tk = vt.shape
    d = n_heads * (LANES // 2)
    assert TQ % (2 * tk) == 0 and s % TQ == 0 and tk == QSTRIP
    return pl.pallas_call(
        _flash_kernel,
        grid=(n_heads // 2, s // TQ),
        in_specs=[
            pl.BlockSpec((2, TQ, LANES), lambda hp, qi: (hp, qi, 0)),
            pl.BlockSpec((2, s, LANES), lambda hp, qi: (hp, 0, 0)),
            pl.BlockSpec((n_kv, 2, vt_rows, tk), lambda hp, qi: (0, hp, 0, 0)),
        ],
        out_specs=pl.BlockSpec((TQ, LANES), lambda hp, qi: (qi, hp)),
        out_shape=jax.ShapeDtypeStruct((s, d), BF16),
        scratch_shapes=[pltpu.VMEM((2, 2, tk, TQ), F32), pltpu.VMEM((2, vt_rows, TQ), F32)],
        compiler_params=pltpu.CompilerParams(
            dimension_semantics=("arbitrary", "arbitrary"), vmem_limit_bytes=VMEM_LIMIT_BYTES),
        name="flash_attn",
    )(q_aug, k_aug, vt)


def _rglru_kernel(x_ref, wx_ref, wg_ref, cw_ref, cb_ref, wa_ref, ba_ref, wi_ref, bi_ref,
                  lam_ref, wo_ref, g_ref, b_ref, y_ref, xprev_ref, hprev_ref):
    tm, d = x_ref.shape
    n_blocks = wa_ref.shape[0]
    blk = d // n_blocks
    n_taps = cw_ref.shape[0]

    @pl.when(pl.program_id(0) == 0)
    def _():
        xprev_ref[...] = jnp.zeros_like(xprev_ref)
        hprev_ref[...] = jnp.zeros_like(hprev_ref)

    x = x_ref[...]
    xb = x.astype(BF16)
    xp = _dot(xb, wx_ref[...])
    gb = _dot(xb, wg_ref[...])

    prev8 = xprev_ref[...]
    xc = cb_ref[...] + cw_ref[n_taps - 1:n_taps, :] * xp
    for k in range(1, n_taps):
        xc = xc + cw_ref[n_taps - 1 - k:n_taps - k, :] * _shift_rows(prev8, xp, k)
    xprev_ref[...] = xp[tm - SUBLANES:, :]

    r_parts, i_parts = [], []
    for n in range(n_blocks):
        xh = xc[:, n * blk:(n + 1) * blk].astype(BF16)
        r_parts.append(_dot(xh, wa_ref[n]))
        i_parts.append(_dot(xh, wi_ref[n]))
    r = _sigmoid(jnp.concatenate(r_parts, axis=1) + ba_ref[...])
    ig = _sigmoid(jnp.concatenate(i_parts, axis=1) + bi_ref[...])

    log_a = (-LRU_C) * r * jax.nn.softplus(-lam_ref[...])
    a = jnp.exp(log_a)
    z = -jnp.tanh(log_a) * (a * a + 1.0)
    u = jnp.where(z > 0.0, z * lax.rsqrt(z), 0.0) * (ig * xc)

    n_groups = tm // SUBLANES
    a = a.reshape(n_groups, SUBLANES, d)
    u = u.reshape(n_groups, SUBLANES, d)
    sub = lax.broadcasted_iota(jnp.int32, (1, SUBLANES, d), 1)
    dist = 1
    while dist < SUBLANES:
        a_sh = jnp.where(sub >= dist, pltpu.roll(a, dist, 1), 1.0)
        u_sh = jnp.where(sub >= dist, pltpu.roll(u, dist, 1), 0.0)
        u = u + a * u_sh
        a = a * a_sh
        dist *= 2
    h_last = hprev_ref[0:1, :]
    groups = []
    for g in range(n_groups):
        h_g = u[g] + a[g] * h_last
        groups.append(h_g)
        h_last = h_g[SUBLANES - 1:SUBLANES, :]
    h = jnp.concatenate(groups, axis=0)
    hprev_ref[...] = jnp.broadcast_to(h_last, hprev_ref.shape)

    yv = (h * jax.nn.gelu(gb)).astype(BF16)
    m = _dot(yv, wo_ref[...])
    y_ref[...] = _layer_norm(ALPHA * x + m, g_ref[...], b_ref[...])


def _rglru_layer(x, w_in, conv_w, conv_b, w_a, b_a, w_i, b_i, lam, w_out, j, g, b, i):
    s, d = x.shape
    _, n_blocks, blk, _ = w_a.shape
    n_taps = conv_w.shape[1]
    assert s % TM_LRU == 0 and n_taps - 1 < SUBLANES
    vec = _layer_block((1, d), j, 0, 0)
    return pl.pallas_call(
        _rglru_kernel,
        grid=(s // TM_LRU,),
        in_specs=[
            pl.BlockSpec((TM_LRU, d), lambda r: (r, 0)),
            _layer_block((d, d), j, 0, 0), _layer_block((d, d), j, 0, 1),
            _layer_block((n_taps, d), j, 0, 0), vec,
            _layer_block((n_blocks, blk, blk), j, 0, 0, 0), vec,
            _layer_block((n_blocks, blk, blk), j, 0, 0, 0), vec,
            vec,
            _layer_block((d, d), j, 0, 0),
            _layer_block((1, d), i, 0, 0), _layer_block((1, d), i, 0, 0),
        ],
        out_specs=pl.BlockSpec((TM_LRU, d), lambda i: (i, 0)),
        out_shape=jax.ShapeDtypeStruct((s, d), F32),
        scratch_shapes=[pltpu.VMEM((SUBLANES, d), F32), pltpu.VMEM((SUBLANES, d), F32)],
        compiler_params=pltpu.CompilerParams(
            dimension_semantics=("arbitrary",), vmem_limit_bytes=VMEM_LIMIT_BYTES),
        name="rglru",
    )(x, w_in, w_in, conv_w, conv_b, w_a, b_a, w_i, b_i, lam, w_out, g, b)


def _ffn_kernel(*refs, attn_out):
    if attn_out:
        o_ref, wo_ref, g1_ref, b1_ref, *refs = refs
    (x_ref, wv_ref, wg_ref, cw_ref, cb_ref, wd_ref, g_ref, b_ref,
     p_ref, wp_ref, wpg_ref, bpg_ref, y_ref, prev_ref, h_ref, act_ref) = refs
    tm, d = x_ref.shape
    d_ff = wd_ref.shape[0]
    tn = h_ref.shape[3]
    n_chunks = d_ff // tn
    n_taps = cw_ref.shape[0]

    @pl.when(pl.program_id(0) == 0)
    def _():
        prev_ref[...] = jnp.zeros_like(prev_ref)

    x = x_ref[...]
    if attn_out:
        x = _layer_norm(ALPHA * x + _dot(o_ref[...], wo_ref[...]), g1_ref[...], b1_ref[...])
    xb = x.astype(BF16)

    def up(slot, j):
        cols = slice(j * tn, (j + 1) * tn)
        h_ref[slot, 0] = _dot(xb, wv_ref[:, cols])
        h_ref[slot, 1] = _dot(xb, wg_ref[:, cols])

    def conv(slot, part, j):
        cols = slice(part * d_ff + j * tn, part * d_ff + (j + 1) * tn)
        h = h_ref[slot, part]
        prev8 = prev_ref[:, cols]
        out = cb_ref[:, cols] + cw_ref[n_taps - 1:n_taps, cols] * h
        for k in range(1, n_taps):
            out = out + cw_ref[n_taps - 1 - k:n_taps - k, cols] * _shift_rows(prev8, h, k)
        prev_ref[:, cols] = h[tm - SUBLANES:, :]
        return out

    up(0, 0)
    for j in range(n_chunks):
        if j + 1 < n_chunks:
            up((j + 1) % 2, j + 1)
        val = conv(j % 2, 0, j)
        gate = conv(j % 2, 1, j)
        act_ref[:, j * tn:(j + 1) * tn] = (jax.nn.gelu(gate) * val).astype(BF16)

    ff = _dot(act_ref[...], wd_ref[...])
    x2 = _layer_norm(ALPHA * x + ff, g_ref[...], b_ref[...])
    gate = _sigmoid(_dot(x2.astype(BF16), wpg_ref[...]) + bpg_ref[...])
    y_ref[...] = x2 + gate * _dot(p_ref[...].astype(BF16), wp_ref[...])


def _ffn_layer(x, w_up, conv_w, conv_b, w_down, g, b, p, ple_w, ple_gate_w, ple_gate_b, i, bi,
               attn=None):
    s, d = x.shape
    d_ff = w_down.shape[1]
    d_ple = p.shape[3]
    n_taps = conv_w.shape[1]
    assert s % TM_FFN == 0 and d_ff % TN_FFN == 0 and n_taps - 1 < SUBLANES
    vec = _layer_block((1, d), i, 0, 0)
    attn_specs, attn_args = [], ()
    if attn is not None:
        o, w_out, j, g1, b1 = attn
        attn_specs = [pl.BlockSpec((TM_FFN, d), lambda r: (r, 0)), _layer_block((d, d), j, 0, 0),
                      vec, vec]
        attn_args = (o, w_out, g1, b1)
    return pl.pallas_call(
        functools.partial(_ffn_kernel, attn_out=attn is not None),
        grid=(s // TM_FFN,),
        in_specs=attn_specs + [
            pl.BlockSpec((TM_FFN, d), lambda r: (r, 0)),
            _layer_block((d, d_ff), i, 0, 0), _layer_block((d, d_ff), i, 0, 1),
            _layer_block((n_taps, 2 * d_ff), i, 0, 0), _layer_block((1, 2 * d_ff), i, 0, 0),
            _layer_block((d_ff, d), i, 0, 0), vec, vec,
            pl.BlockSpec((None, None, TM_FFN, d_ple), lambda r: (i, bi, r, 0)),
            _layer_block((d_ple, d), i, 0, 0), _layer_block((d, d), i, 0, 0), vec,
        ],
        out_specs=pl.BlockSpec((TM_FFN, d), lambda i: (i, 0)),
        out_shape=jax.ShapeDtypeStruct((s, d), F32),
        scratch_shapes=[pltpu.VMEM((SUBLANES, 2 * d_ff), F32),
                        pltpu.VMEM((2, 2, TM_FFN, TN_FFN), F32),
                        pltpu.VMEM((TM_FFN, d_ff), BF16)],
        compiler_params=pltpu.CompilerParams(
            dimension_semantics=("arbitrary",), vmem_limit_bytes=VMEM_LIMIT_BYTES),
        name="conv_ffn",
    )(*attn_args, x, w_up, w_up, conv_w, conv_b, w_down, g, b, p, ple_w, ple_gate_w, ple_gate_b)


def kernel(x, p, a_w_in, a_b_f, a_w_out, b_w_in, b_conv_w, b_conv_b, b_w_a, b_b_a, b_w_i, b_b_i,
           b_lam, b_w_out, f_w_up, f_conv_w, f_conv_b, f_w_down, ln1_g, ln1_b, ln2_g, ln2_b,
           ple_w, ple_gate_w, ple_gate_b):
    bsz, s, d = x.shape
    depth = p.shape[0]
    n_heads = a_b_f.shape[1]
    bf16 = lambda w: w.astype(BF16)
    attn_w = _attn_weights(a_w_in, a_b_f)
    a_w_out, b_w_in, b_w_a, b_w_i, b_w_out = map(bf16, (a_w_out, b_w_in, b_w_a, b_w_i, b_w_out))
    f_w_up, f_w_down, ple_w, ple_gate_w = map(bf16, (f_w_up, f_w_down, ple_w, ple_gate_w))
    b_b_a, b_b_i = (v.reshape(v.shape[0], 1, d) for v in (b_b_a, b_b_i))
    b_conv_b, b_lam, f_conv_b, ple_gate_b = map(_rows, (b_conv_b, b_lam, f_conv_b, ple_gate_b))
    ln1_g, ln1_b, ln2_g, ln2_b = map(_rows, (ln1_g, ln1_b, ln2_g, ln2_b))
    outs = []
    for bi in range(bsz):
        xs = x[bi]
        for i in range(depth):
            j = i // 2
            attn = None
            if i % 2 == 0:
                q_aug, k_aug, vt = _attn_proj(xs, attn_w, j, n_heads)
                attn = (_flash(q_aug, k_aug, vt), a_w_out, j, ln1_g, ln1_b)
            else:
                xs = _rglru_layer(xs, b_w_in, b_conv_w, b_conv_b, b_w_a, b_b_a, b_w_i, b_b_i,
                                  b_lam, b_w_out, j, ln1_g, ln1_b, i)
            xs = _ffn_layer(xs, f_w_up, f_conv_w, f_conv_b, f_w_down, ln2_g, ln2_b, p,
                            ple_w, ple_gate_w, ple_gate_b, i, bi, attn)
        outs.append(xs)
    return jnp.stack(outs, axis=0)
```

```python
import functools
import math

import jax
import jax.numpy as jnp
from jax import lax
from jax.experimental import pallas as pl
from jax.experimental.pallas import tpu as pltpu

N_HEADS = 16
LRU_C = 8.0
LN_EPS = 1e-5
DEPTH = 4
ALPHA = (2.0 * DEPTH) ** 0.25
LOG2E = math.log2(math.e)

LANES = 128
SUBLANES = 8
VMEM_LIMIT_BYTES = 56 * 1024 * 1024

TQ = 2048
TK = 256
QSTRIP = 256
TM_PROJ = 512
TM_FFN = 512
TN_FFN = 256
TM_LRU = 256

AUG_WIDTH = 6
VT_ROWS = LANES // 2 + 2 * SUBLANES
NEG_BIG = -1e30

BF16 = jnp.bfloat16
F32 = jnp.float32
NT_DIMS = (((1,), (1,)), ((), ()))


def _resident(shape):
    n = len(shape)
    return pl.BlockSpec(shape, lambda *_: (0,) * n, pipeline_mode=pl.Buffered(1))


def _layer_block(block, *index):
    return pl.BlockSpec((None,) + tuple(block), lambda *_: tuple(index),
                        pipeline_mode=pl.Buffered(1))


def _rows(v):
    return v.reshape(v.shape[0], 1, -1)


def _dot(a, b):
    return jnp.dot(a, b, preferred_element_type=F32)


def _layer_norm(y, g, b):
    mu = jnp.mean(y, axis=-1, keepdims=True)
    yc = y - mu
    var = jnp.mean(yc * yc, axis=-1, keepdims=True)
    return yc * lax.rsqrt(var + LN_EPS) * g + b


def _sigmoid(x):
    return 0.5 * jnp.tanh(0.5 * x) + 0.5


def _split3(x):
    p1 = x.astype(BF16)
    r1 = x - p1.astype(F32)
    p2 = r1.astype(BF16)
    p3 = (r1 - p2.astype(F32)).astype(BF16)
    return p1, p2, p3


def _shift_rows(prev8, cur, k):
    both = jnp.concatenate([prev8, cur], axis=0)
    return pltpu.roll(both, k, 0)[SUBLANES:]


def _attn_proj_kernel(x_ref, wq_ref, wk_ref, wvt_ref, wf_ref, bf_ref, tri_ref, pq_ref, pk_ref,
                      q_ref, k_ref, vt_ref, carry_ref, *, n_heads, scale):
    tm = x_ref.shape[0]

    @pl.when(pl.program_id(0) == 0)
    def _():
        carry_ref[...] = jnp.zeros_like(carry_ref)

    xb = x_ref[...].astype(BF16)
    lane = lax.broadcasted_iota(jnp.int32, (tm, LANES), 1)

    fg = _dot(xb, wf_ref[...]) + bf_ref[...]
    lf = (jnp.minimum(fg, 0.0) - jnp.log1p(jnp.exp(-jnp.abs(fg)))) * LOG2E
    lf = jnp.where(lane < n_heads, lf, 0.0)

    cs = _dot(tri_ref[...], jnp.concatenate(_split3(lf), axis=1))
    c = cs[:, :LANES] + cs[:, LANES:2 * LANES] + cs[:, 2 * LANES:] + carry_ref[0:1, :]
    carry_ref[...] = jnp.broadcast_to(c[tm - 1:tm, :], carry_ref.shape)

    c1, c2, c3 = _split3(c)
    cz = (c1.astype(F32) + pltpu.roll(c2.astype(F32), n_heads, 1)
          + pltpu.roll(c3.astype(F32), 2 * n_heads, 1))
    cz = jnp.where(lane == 3 * n_heads, 1.0, cz).astype(BF16)
    eq = _dot(cz, pq_ref[...])
    ek = _dot(cz, pk_ref[...])

    qf = _dot(xb, wq_ref[...]) * (scale * LOG2E)
    kf = _dot(xb, wk_ref[...])
    half = LANES // 2
    for pair in range(n_heads // 2):
        cols = slice(pair * LANES, (pair + 1) * LANES)
        q_ref[2 * pair] = jnp.where(lane < half, qf[:, cols], eq[:, cols]).astype(BF16)
        q_ref[2 * pair + 1] = jnp.where(lane >= half, qf[:, cols], eq[:, cols]).astype(BF16)
        k_ref[2 * pair] = jnp.where(lane < half, kf[:, cols], ek[:, cols]).astype(BF16)
        k_ref[2 * pair + 1] = jnp.where(lane >= half, kf[:, cols], ek[:, cols]).astype(BF16)

    vt = lax.dot_general(wvt_ref[...], xb, NT_DIMS, preferred_element_type=F32).astype(BF16)
    dh = LANES // 2
    tk = vt_ref.shape[3]
    ones_rows = jnp.where(lax.broadcasted_iota(jnp.int32, (VT_ROWS - dh, tk), 0) == 0, 1.0, 0.0)
    for blk in range(tm // tk):
        for h in range(n_heads):
            vt_ref[blk, h, :dh, :] = vt[h * dh:(h + 1) * dh, blk * tk:(blk + 1) * tk]
            vt_ref[blk, h, dh:, :] = ones_rows.astype(BF16)


def _placement_matrices(n_heads):
    import numpy as np
    pq = np.zeros((LANES, n_heads // 2 * LANES), np.float32)
    pk = np.zeros((LANES, n_heads // 2 * LANES), np.float32)
    one_lane = 3 * n_heads
    for h in range(n_heads):
        base = (h // 2) * LANES + (LANES // 2 if h % 2 == 0 else 0)
        for piece in range(3):
            pq[piece * n_heads + h, base + piece] = 1.0
            pq[one_lane, base + 3 + piece] = 1.0
            pk[one_lane, base + piece] = 1.0
            pk[piece * n_heads + h, base + 3 + piece] = -1.0
    return jnp.asarray(pq, BF16), jnp.asarray(pk, BF16)


def _attn_weights(a_w_in, a_b_f):
    n_a, d, _ = a_w_in.shape
    n_heads = a_b_f.shape[1]
    wvt = jnp.swapaxes(a_w_in[:, :, 2 * d:3 * d], 1, 2).astype(BF16)
    wf = jnp.pad(a_w_in[:, :, 3 * d:], ((0, 0), (0, 0), (0, LANES - n_heads))).astype(BF16)
    bf = jnp.pad(a_b_f, ((0, 0), (0, LANES - n_heads))).reshape(n_a, 1, LANES)
    return a_w_in.astype(BF16), wvt, wf, bf


def _attn_proj(x, weights, j, n_heads):
    w_in, wvt, wf, bf = weights
    s, d = x.shape
    dh = d // n_heads
    assert 2 * dh == LANES and 3 * n_heads < LANES and s % TM_PROJ == 0
    tri = jnp.tril(jnp.ones((TM_PROJ, TM_PROJ), BF16))
    pq, pk = _placement_matrices(n_heads)
    kern = functools.partial(_attn_proj_kernel, n_heads=n_heads, scale=1.0 / math.sqrt(dh))
    return pl.pallas_call(
        kern,
        grid=(s // TM_PROJ,),
        in_specs=[
            pl.BlockSpec((TM_PROJ, d), lambda i: (i, 0)),
            _layer_block((d, d), j, 0, 0), _layer_block((d, d), j, 0, 1),
            _layer_block((d, d), j, 0, 0),
            _layer_block((d, LANES), j, 0, 0), _layer_block((1, LANES), j, 0, 0),
            _resident((TM_PROJ, TM_PROJ)),
            _resident(pq.shape), _resident(pk.shape),
        ],
        out_specs=[
            pl.BlockSpec((n_heads, TM_PROJ, LANES), lambda i: (0, i, 0)),
            pl.BlockSpec((n_heads, TM_PROJ, LANES), lambda i: (0, i, 0)),
            pl.BlockSpec((TM_PROJ // TK, n_heads, VT_ROWS, TK), lambda i: (i, 0, 0, 0)),
        ],
        out_shape=[
            jax.ShapeDtypeStruct((n_heads, s, LANES), BF16),
            jax.ShapeDtypeStruct((n_heads, s, LANES), BF16),
            jax.ShapeDtypeStruct((s // TK, n_heads, VT_ROWS, TK), BF16),
        ],
        scratch_shapes=[pltpu.VMEM((SUBLANES, LANES), F32)],
        compiler_params=pltpu.CompilerParams(
            dimension_semantics=("arbitrary",), vmem_limit_bytes=VMEM_LIMIT_BYTES),
        name="attn_proj",
    )(x, w_in, w_in, wvt, wf, bf, tri, pq, pk)


def _flash_kernel(q_ref, k_ref, vt_ref, o_ref, s_ref, acc_ref):
    tq = q_ref.shape[1]
    tk = vt_ref.shape[3]
    dh = LANES // 2
    qi = pl.program_id(1)
    qs = (q_ref[0], q_ref[1])

    tri = (lax.broadcasted_iota(jnp.int32, (tk, QSTRIP), 0)
           <= lax.broadcasted_iota(jnp.int32, (tk, QSTRIP), 1))
    n_strips = tq // QSTRIP

    def step(cur, nxt, ms, maxima):
        slot, j, c0, consume_masked = cur
        new_ms, new_maxima = [], []
        for hh in range(2):
            m = ms[hh]
            if nxt is not None:
                nslot, nj, nc0, store_masked = nxt
                k_next = k_ref[hh, pl.ds(pl.multiple_of(nj * tk, tk), tk), :]
            m_parts = [m[:, :c0]] if c0 else []
            mx = [None] * n_strips
            for si in range(n_strips):
                q0 = si * QSTRIP
                cols = slice(q0, q0 + QSTRIP)
                if nxt is not None and q0 >= nc0:
                    s_n = lax.dot_general(k_next, qs[hh][cols], NT_DIMS,
                                          preferred_element_type=F32)
                    if store_masked and q0 == nc0:
                        s_n = jnp.where(tri, s_n, NEG_BIG)
                    s_ref[nslot, hh, :, cols] = s_n
                    mx[si] = jnp.max(s_n, axis=0, keepdims=True)
                if q0 >= c0:
                    s = s_ref[slot, hh, :, cols]
                    if consume_masked and q0 == c0:
                        s = jnp.where(tri, s, NEG_BIG)
                        s_max = jnp.max(s, axis=0, keepdims=True)
                    else:
                        s_max = maxima[hh][si]
                    m_new = jnp.maximum(m[:, cols], s_max)
                    p = jnp.exp2(s - m_new).astype(BF16)
                    acc_ref[hh, :, cols] = (jnp.exp2(m[:, cols] - m_new) * acc_ref[hh, :, cols]
                                            + _dot(vt_ref[j, hh], p))
                    m_parts.append(m_new)
            new_ms.append(jnp.concatenate(m_parts, axis=1))
            new_maxima.append(tuple(mx))
        return tuple(new_ms), tuple(new_maxima)

    def first_scores():
        maxima = []
        for hh in range(2):
            s = lax.dot_general(k_ref[hh, 0:tk, :], qs[hh], NT_DIMS, preferred_element_type=F32)
            s_ref[0, hh] = s
            mx = jnp.max(s, axis=0, keepdims=True)
            maxima.append(tuple(mx[:, si * QSTRIP:(si + 1) * QSTRIP] for si in range(n_strips)))
        return tuple(maxima)

    n_sub = tq // tk

    def body(jj, carry):
        ms, maxima = carry
        for b in range(n_sub):
            j = n_sub * jj + b
            ms, maxima = step((b % 2, j, 0, False), ((b + 1) % 2, j + 1, 0, False), ms, maxima)
        return ms, maxima

    acc_ref[...] = jnp.zeros(acc_ref.shape, F32)
    m0 = jnp.full((1, tq), NEG_BIG, F32)
    ms, maxima = lax.fori_loop(0, qi, body, ((m0, m0), first_scores()))
    for b in range(n_sub):
        j = n_sub * qi + b
        nxt = ((b + 1) % 2, j + 1, (b + 1) * tk, True) if b + 1 < n_sub else None
        ms, maxima = step((b % 2, j, b * tk, b == 0), nxt, ms, maxima)
    outs = [acc_ref[hh, :dh, :] / acc_ref[hh, dh:dh + 1, :] for hh in range(2)]
    o_ref[...] = jnp.concatenate(outs, axis=0).T.astype(o_ref.dtype)


def _flash(q_aug, k_aug, vt):
    n_heads, s, _ = q_aug.shape
    n_kv, _, vt_rows, tk = vt.shape
    d = n_heads * (LANES // 2)
    assert TQ % (2 * tk) == 0 and s % TQ == 0 and tk == QSTRIP
    return pl.pallas_call(
        _flash_kernel,
        grid=(n_heads // 2, s // TQ),
        in_specs=[
            pl.BlockSpec((2, TQ, LANES), lambda hp, qi: (hp, qi, 0)),
            pl.BlockSpec((2, s, LANES), lambda hp, qi: (hp, 0, 0)),
            pl.BlockSpec((n_kv, 2, vt_rows, tk), lambda hp, qi: (0, hp, 0, 0)),
        ],
        out_specs=pl.BlockSpec((TQ, LANES), lambda hp, qi: (qi, hp)),
        out_shape=jax.ShapeDtypeStruct((s, d), BF16),
        scratch_shapes=[pltpu.VMEM((2, 2, tk, TQ), F32), pltpu.VMEM((2, vt_rows, TQ), F32)],
        compiler_params=pltpu.CompilerParams(
            dimension_semantics=("arbitrary", "arbitrary"), vmem_limit_bytes=VMEM_LIMIT_BYTES),
        name="flash_attn",
    )(q_aug, k_aug, vt)


def _rglru_kernel(x_ref, wx_ref, wg_ref, cw_ref, cb_ref, wa_ref, ba_ref, wi_ref, bi_ref,
                  lam_ref, wo_ref, g_ref, b_ref, y_ref, xprev_ref, hprev_ref):
    tm, d = x_ref.shape
    n_blocks = wa_ref.shape[0]
    blk = d // n_blocks
    n_taps = cw_ref.shape[0]

    @pl.when(pl.program_id(0) == 0)
    def _():
        xprev_ref[...] = jnp.zeros_like(xprev_ref)
        hprev_ref[...] = jnp.zeros_like(hprev_ref)

    x = x_ref[...]
    xb = x.astype(BF16)
    xp = _dot(xb, wx_ref[...])
    gb = _dot(xb, wg_ref[...])

    prev8 = xprev_ref[...]
    xc = cb_ref[...] + cw_ref[n_taps - 1:n_taps, :] * xp
    for k in range(1, n_taps):
        xc = xc + cw_ref[n_taps - 1 - k:n_taps - k, :] * _shift_rows(prev8, xp, k)
    xprev_ref[...] = xp[tm - SUBLANES:, :]

    r_parts, i_parts = [], []
    for n in range(n_blocks):
        xh = xc[:, n * blk:(n + 1) * blk].astype(BF16)
        r_parts.append(_dot(xh, wa_ref[n]))
        i_parts.append(_dot(xh, wi_ref[n]))
    r = _sigmoid(jnp.concatenate(r_parts, axis=1) + ba_ref[...])
    ig = _sigmoid(jnp.concatenate(i_parts, axis=1) + bi_ref[...])

    log_a = (-LRU_C) * r * jax.nn.softplus(-lam_ref[...])
    a = jnp.exp(log_a)
    z = -jnp.tanh(log_a) * (a * a + 1.0)
    u = jnp.where(z > 0.0, z * lax.rsqrt(z), 0.0) * (ig * xc)

    n_groups = tm // SUBLANES
    a = a.reshape(n_groups, SUBLANES, d)
    u = u.reshape(n_groups, SUBLANES, d)
    sub = lax.broadcasted_iota(jnp.int32, (1, SUBLANES, d), 1)
    dist = 1
    while dist < SUBLANES:
        a_sh = jnp.where(sub >= dist, pltpu.roll(a, dist, 1), 1.0)
        u_sh = jnp.where(sub >= dist, pltpu.roll(u, dist, 1), 0.0)
        u = u + a * u_sh
        a = a * a_sh
        dist *= 2
    h_last = hprev_ref[0:1, :]
    groups = []
    for g in range(n_groups):
        h_g = u[g] + a[g] * h_last
        groups.append(h_g)
        h_last = h_g[SUBLANES - 1:SUBLANES, :]
    h = jnp.concatenate(groups, axis=0)
    hprev_ref[...] = jnp.broadcast_to(h_last, hprev_ref.shape)

    yv = (h * jax.nn.gelu(gb)).astype(BF16)
    m = _dot(yv, wo_ref[...])
    y_ref[...] = _layer_norm(ALPHA * x + m, g_ref[...], b_ref[...])


def _rglru_layer(x, w_in, conv_w, conv_b, w_a, b_a, w_i, b_i, lam, w_out, j, g, b, i):
    s, d = x.shape
    _, n_blocks, blk, _ = w_a.shape
    n_taps = conv_w.shape[1]
    assert s % TM_LRU == 0 and n_taps - 1 < SUBLANES
    vec = _layer_block((1, d), j, 0, 0)
    return pl.pallas_call(
        _rglru_kernel,
        grid=(s // TM_LRU,),
        in_specs=[
            pl.BlockSpec((TM_LRU, d), lambda r: (r, 0)),
            _layer_block((d, d), j, 0, 0), _layer_block((d, d), j, 0, 1),
            _layer_block((n_taps, d), j, 0, 0), vec,
            _layer_block((n_blocks, blk, blk), j, 0, 0, 0), vec,
            _layer_block((n_blocks, blk, blk), j, 0, 0, 0), vec,
            vec,
            _layer_block((d, d), j, 0, 0),
            _layer_block((1, d), i, 0, 0), _layer_block((1, d), i, 0, 0),
        ],
        out_specs=pl.BlockSpec((TM_LRU, d), lambda i: (i, 0)),
        out_shape=jax.ShapeDtypeStruct((s, d), F32),
        scratch_shapes=[pltpu.VMEM((SUBLANES, d), F32), pltpu.VMEM((SUBLANES, d), F32)],
        compiler_params=pltpu.CompilerParams(
            dimension_semantics=("arbitrary",), vmem_limit_bytes=VMEM_LIMIT_BYTES),
        name="rglru",
    )(x, w_in, w_in, conv_w, conv_b, w_a, b_a, w_i, b_i, lam, w_out, g, b)


def _ffn_kernel(*refs, attn_out):
    if attn_out:
        o_ref, wo_ref, g1_ref, b1_ref, *refs = refs
    (x_ref, wv_ref, wg_ref, cw_ref, cb_ref, wd_ref, g_ref, b_ref,
     p_ref, wp_ref, wpg_ref, bpg_ref, y_ref, prev_ref, h_ref, act_ref) = refs
    tm, d = x_ref.shape
    d_ff = wd_ref.shape[0]
    tn = h_ref.shape[3]
    n_chunks = d_ff // tn
    n_taps = cw_ref.shape[0]

    @pl.when(pl.program_id(0) == 0)
    def _():
        prev_ref[...] = jnp.zeros_like(prev_ref)

    x = x_ref[...]
    if attn_out:
        x = _layer_norm(ALPHA * x + _dot(o_ref[...], wo_ref[...]), g1_ref[...], b1_ref[...])
    xb = x.astype(BF16)

    def up(slot, j):
        cols = slice(j * tn, (j + 1) * tn)
        h_ref[slot, 0] = _dot(xb, wv_ref[:, cols])
        h_ref[slot, 1] = _dot(xb, wg_ref[:, cols])

    def conv(slot, part, j):
        cols = slice(part * d_ff + j * tn, part * d_ff + (j + 1) * tn)
        h = h_ref[slot, part]
        prev8 = prev_ref[:, cols]
        out = cb_ref[:, cols] + cw_ref[n_taps - 1:n_taps, cols] * h
        for k in range(1, n_taps):
            out = out + cw_ref[n_taps - 1 - k:n_taps - k, cols] * _shift_rows(prev8, h, k)
        prev_ref[:, cols] = h[tm - SUBLANES:, :]
        return out

    up(0, 0)
    for j in range(n_chunks):
        if j + 1 < n_chunks:
            up((j + 1) % 2, j + 1)
        val = conv(j % 2, 0, j)
        gate = conv(j % 2, 1, j)
        act_ref[:, j * tn:(j + 1) * tn] = (jax.nn.gelu(gate) * val).astype(BF16)

    ff = _dot(act_ref[...], wd_ref[...])
    x2 = _layer_norm(ALPHA * x + ff, g_ref[...], b_ref[...])
    gate = _sigmoid(_dot(x2.astype(BF16), wpg_ref[...]) + bpg_ref[...])
    y_ref[...] = x2 + gate * _dot(p_ref[...].astype(BF16), wp_ref[...])


def _ffn_layer(x, w_up, conv_w, conv_b, w_down, g, b, p, ple_w, ple_gate_w, ple_gate_b, i, bi,
               attn=None):
    s, d = x.shape
    d_ff = w_down.shape[1]
    d_ple = p.shape[3]
    n_taps = conv_w.shape[1]
    assert s % TM_FFN == 0 and d_ff % TN_FFN == 0 and n_taps - 1 < SUBLANES
    vec = _layer_block((1, d), i, 0, 0)
    attn_specs, attn_args = [], ()
    if attn is not None:
        o, w_out, j, g1, b1 = attn
        attn_specs = [pl.BlockSpec((TM_FFN, d), lambda r: (r, 0)), _layer_block((d, d), j, 0, 0),
                      vec, vec]
        attn_args = (o, w_out, g1, b1)
    return pl.pallas_call(
        functools.partial(_ffn_kernel, attn_out=attn is not None),
        grid=(s // TM_FFN,),
        in_specs=attn_specs + [
            pl.BlockSpec((TM_FFN, d), lambda r: (r, 0)),
            _layer_block((d, d_ff), i, 0, 0), _layer_block((d, d_ff), i, 0, 1),
            _layer_block((n_taps, 2 * d_ff), i, 0, 0), _layer_block((1, 2 * d_ff), i, 0, 0),
            _layer_block((d_ff, d), i, 0, 0), vec, vec,
            pl.BlockSpec((None, None, TM_FFN, d_ple), lambda r: (i, bi, r, 0)),
            _layer_block((d_ple, d), i, 0, 0), _layer_block((d, d), i, 0, 0), vec,
        ],
        out_specs=pl.BlockSpec((TM_FFN, d), lambda i: (i, 0)),
        out_shape=jax.ShapeDtypeStruct((s, d), F32),
        scratch_shapes=[pltpu.VMEM((SUBLANES, 2 * d_ff), F32),
                        pltpu.VMEM((2, 2, TM_FFN, TN_FFN), F32),
                        pltpu.VMEM((TM_FFN, d_ff), BF16)],
        compiler_params=pltpu.CompilerParams(
            dimension_semantics=("arbitrary",), vmem_limit_bytes=VMEM_LIMIT_BYTES),
        name="conv_ffn",
    )(*attn_args, x, w_up, w_up, conv_w, conv_b, w_down, g, b, p, ple_w, ple_gate_w, ple_gate_b)


def kernel(x, p, a_w_in, a_b_f, a_w_out, b_w_in, b_conv_w, b_conv_b, b_w_a, b_b_a, b_w_i, b_b_i,
           b_lam, b_w_out, f_w_up, f_conv_w, f_conv_b, f_w_down, ln1_g, ln1_b, ln2_g, ln2_b,
           ple_w, ple_gate_w, ple_gate_b):
    bsz, s, d = x.shape
    depth = p.shape[0]
    n_heads = a_b_f.shape[1]
    bf16 = lambda w: w.astype(BF16)
    attn_w = _attn_weights(a_w_in, a_b_f)
    a_w_out, b_w_in, b_w_a, b_w_i, b_w_out = map(bf16, (a_w_out, b_w_in, b_w_a, b_w_i, b_w_out))
    f_w_up, f_w_down, ple_w, ple_gate_w = map(bf16, (f_w_up, f_w_down, ple_w, ple_gate_w))
    b_b_a, b_b_i = (v.reshape(v.shape[0], 1, d) for v in (b_b_a, b_b_i))
    b_conv_b, b_lam, f_conv_b, ple_gate_b = map(_rows, (b_conv_b, b_lam, f_conv_b, ple_gate_b))
    ln1_g, ln1_b, ln2_g, ln2_b = map(_rows, (ln1_g, ln1_b, ln2_g, ln2_b))
    outs = []
    for bi in range(bsz):
        xs = x[bi]
        for i in range(depth):
            j = i // 2
            attn = None
            if i % 2 == 0:
                q_aug, k_aug, vt = _attn_proj(xs, attn_w, j, n_heads)
                attn = (_flash(q_aug, k_aug, vt), a_w_out, j, ln1_g, ln1_b)
            else:
                xs = _rglru_layer(xs, b_w_in, b_conv_w, b_conv_b, b_w_a, b_b_a, b_w_i, b_b_i,
                                  b_lam, b_w_out, j, ln1_g, ln1_b, i)
            xs = _ffn_layer(xs, f_w_up, f_conv_w, f_conv_b, f_w_down, ln2_g, ln2_b, p,
                            ple_w, ple_gate_w, ple_gate_b, i, bi, attn)
        outs.append(xs)
    return jnp.stack(outs, axis=0)
```

```python
import functools
import math

import jax
import jax.numpy as jnp
from jax import lax
from jax.experimental import pallas as pl
from jax.experimental.pallas import tpu as pltpu

N_HEADS = 16
LRU_C = 8.0
LN_EPS = 1e-5
DEPTH = 4
ALPHA = (2.0 * DEPTH) ** 0.25
LOG2E = math.log2(math.e)

LANES = 128
SUBLANES = 8
VMEM_LIMIT_BYTES = 56 * 1024 * 1024

TQ = 2048
TK = 256
QSTRIP = 256
TM_PROJ = 512
TM_FFN = 512
TN_FFN = 256
FFN_TAIL_ROWS = 256
TM_LRU = 256

AUG_WIDTH = 6
VT_ROWS = LANES // 2 + 2 * SUBLANES
NEG_BIG = -1e30

BF16 = jnp.bfloat16
F32 = jnp.float32
NT_DIMS = (((1,), (1,)), ((), ()))


def _resident(shape):
    n = len(shape)
    return pl.BlockSpec(shape, lambda *_: (0,) * n, pipeline_mode=pl.Buffered(1))


def _layer_block(block, *index):
    return pl.BlockSpec((None,) + tuple(block), lambda *_: tuple(index),
                        pipeline_mode=pl.Buffered(1))


def _rows(v):
    return v.reshape(v.shape[0], 1, -1)


def _dot(a, b):
    return jnp.dot(a, b, preferred_element_type=F32)


def _layer_norm(y, g, b):
    mu = jnp.mean(y, axis=-1, keepdims=True)
    yc = y - mu
    var = jnp.mean(yc * yc, axis=-1, keepdims=True)
    return yc * lax.rsqrt(var + LN_EPS) * g + b


def _sigmoid(x):
    return 0.5 * jnp.tanh(0.5 * x) + 0.5


def _split3(x):
    p1 = x.astype(BF16)
    r1 = x - p1.astype(F32)
    p2 = r1.astype(BF16)
    p3 = (r1 - p2.astype(F32)).astype(BF16)
    return p1, p2, p3


def _shift_rows(prev8, cur, k):
    both = jnp.concatenate([prev8, cur], axis=0)
    return pltpu.roll(both, k, 0)[SUBLANES:]


def _attn_proj_kernel(x_ref, wq_ref, wk_ref, wvt_ref, wf_ref, bf_ref, tri_ref, pq_ref, pk_ref,
                      q_ref, k_ref, vt_ref, carry_ref, *, n_heads, scale):
    tm = x_ref.shape[0]

    @pl.when(pl.program_id(0) == 0)
    def _():
        carry_ref[...] = jnp.zeros_like(carry_ref)

    xb = x_ref[...].astype(BF16)
    lane = lax.broadcasted_iota(jnp.int32, (tm, LANES), 1)

    fg = _dot(xb, wf_ref[...]) + bf_ref[...]
    lf = (jnp.minimum(fg, 0.0) - jnp.log1p(jnp.exp(-jnp.abs(fg)))) * LOG2E
    lf = jnp.where(lane < n_heads, lf, 0.0)

    cs = _dot(tri_ref[...], jnp.concatenate(_split3(lf), axis=1))
    c = cs[:, :LANES] + cs[:, LANES:2 * LANES] + cs[:, 2 * LANES:] + carry_ref[0:1, :]
    carry_ref[...] = jnp.broadcast_to(c[tm - 1:tm, :], carry_ref.shape)

    c1, c2, c3 = _split3(c)
    cz = (c1.astype(F32) + pltpu.roll(c2.astype(F32), n_heads, 1)
          + pltpu.roll(c3.astype(F32), 2 * n_heads, 1))
    cz = jnp.where(lane == 3 * n_heads, 1.0, cz).astype(BF16)
    eq = _dot(cz, pq_ref[...])
    ek = _dot(cz, pk_ref[...])

    qf = _dot(xb, wq_ref[...]) * (scale * LOG2E)
    kf = _dot(xb, wk_ref[...])
    half = LANES // 2
    for pair in range(n_heads // 2):
        cols = slice(pair * LANES, (pair + 1) * LANES)
        q_ref[2 * pair] = jnp.where(lane < half, qf[:, cols], eq[:, cols]).astype(BF16)
        q_ref[2 * pair + 1] = jnp.where(lane >= half, qf[:, cols], eq[:, cols]).astype(BF16)
        k_ref[2 * pair] = jnp.where(lane < half, kf[:, cols], ek[:, cols]).astype(BF16)
        k_ref[2 * pair + 1] = jnp.where(lane >= half, kf[:, cols], ek[:, cols]).astype(BF16)

    vt = lax.dot_general(wvt_ref[...], xb, NT_DIMS, preferred_element_type=F32).astype(BF16)
    dh = LANES // 2
    tk = vt_ref.shape[3]
    ones_rows = jnp.where(lax.broadcasted_iota(jnp.int32, (VT_ROWS - dh, tk), 0) == 0, 1.0, 0.0)
    for blk in range(tm // tk):
        for h in range(n_heads):
            vt_ref[blk, h, :dh, :] = vt[h * dh:(h + 1) * dh, blk * tk:(blk + 1) * tk]
            vt_ref[blk, h, dh:, :] = ones_rows.astype(BF16)


def _placement_matrices(n_heads):
    import numpy as np
    pq = np.zeros((LANES, n_heads // 2 * LANES), np.float32)
    pk = np.zeros((LANES, n_heads // 2 * LANES), np.float32)
    one_lane = 3 * n_heads
    for h in range(n_heads):
        base = (h // 2) * LANES + (LANES // 2 if h % 2 == 0 else 0)
        for piece in range(3):
            pq[piece * n_heads + h, base + piece] = 1.0
            pq[one_lane, base + 3 + piece] = 1.0
            pk[one_lane, base + piece] = 1.0
            pk[piece * n_heads + h, base + 3 + piece] = -1.0
    return jnp.asarray(pq, BF16), jnp.asarray(pk, BF16)


def _attn_weights(a_w_in, a_b_f):
    n_a, d, _ = a_w_in.shape
    n_heads = a_b_f.shape[1]
    wvt = jnp.swapaxes(a_w_in[:, :, 2 * d:3 * d], 1, 2).astype(BF16)
    wf = jnp.pad(a_w_in[:, :, 3 * d:], ((0, 0), (0, 0), (0, LANES - n_heads))).astype(BF16)
    bf = jnp.pad(a_b_f, ((0, 0), (0, LANES - n_heads))).reshape(n_a, 1, LANES)
    return a_w_in.astype(BF16), wvt, wf, bf


def _attn_proj(x, weights, j, n_heads):
    w_in, wvt, wf, bf = weights
    s, d = x.shape
    dh = d // n_heads
    assert 2 * dh == LANES and 3 * n_heads < LANES and s % TM_PROJ == 0
    tri = jnp.tril(jnp.ones((TM_PROJ, TM_PROJ), BF16))
    pq, pk = _placement_matrices(n_heads)
    kern = functools.partial(_attn_proj_kernel, n_heads=n_heads, scale=1.0 / math.sqrt(dh))
    return pl.pallas_call(
        kern,
        grid=(s // TM_PROJ,),
        in_specs=[
            pl.BlockSpec((TM_PROJ, d), lambda i: (i, 0)),
            _layer_block((d, d), j, 0, 0), _layer_block((d, d), j, 0, 1),
            _layer_block((d, d), j, 0, 0),
            _layer_block((d, LANES), j, 0, 0), _layer_block((1, LANES), j, 0, 0),
            _resident((TM_PROJ, TM_PROJ)),
            _resident(pq.shape), _resident(pk.shape),
        ],
        out_specs=[
            pl.BlockSpec((n_heads, TM_PROJ, LANES), lambda i: (0, i, 0)),
            pl.BlockSpec((n_heads, TM_PROJ, LANES), lambda i: (0, i, 0)),
            pl.BlockSpec((TM_PROJ // TK, n_heads, VT_ROWS, TK), lambda i: (i, 0, 0, 0)),
        ],
        out_shape=[
            jax.ShapeDtypeStruct((n_heads, s, LANES), BF16),
            jax.ShapeDtypeStruct((n_heads, s, LANES), BF16),
            jax.ShapeDtypeStruct((s // TK, n_heads, VT_ROWS, TK), BF16),
        ],
        scratch_shapes=[pltpu.VMEM((SUBLANES, LANES), F32)],
        compiler_params=pltpu.CompilerParams(
            dimension_semantics=("arbitrary",), vmem_limit_bytes=VMEM_LIMIT_BYTES),
        name="attn_proj",
    )(x, w_in, w_in, wvt, wf, bf, tri, pq, pk)


def _flash_kernel(q_ref, k_ref, vt_ref, o_ref, s_ref, acc_ref):
    tq = q_ref.shape[1]
    tk = vt_ref.shape[3]
    dh = LANES // 2
    qi = pl.program_id(1)
    qs = (q_ref[0], q_ref[1])

    tri = (lax.broadcasted_iota(jnp.int32, (tk, QSTRIP), 0)
           <= lax.broadcasted_iota(jnp.int32, (tk, QSTRIP), 1))
    n_strips = tq // QSTRIP

    def step(cur, nxt, ms, maxima):
        slot, j, c0, consume_masked = cur
        new_ms, new_maxima = [], []
        for hh in range(2):
            m = ms[hh]
            if nxt is not None:
                nslot, nj, nc0, store_masked = nxt
                k_next = k_ref[hh, pl.ds(pl.multiple_of(nj * tk, tk), tk), :]
            m_parts = [m[:, :c0]] if c0 else []
            mx = [None] * n_strips
            for si in range(n_strips):
                q0 = si * QSTRIP
                cols = slice(q0, q0 + QSTRIP)
                if nxt is not None and q0 >= nc0:
                    s_n = lax.dot_general(k_next, qs[hh][cols], NT_DIMS,
                                          preferred_element_type=F32)
                    if store_masked and q0 == nc0:
                        s_n = jnp.where(tri, s_n, NEG_BIG)
                    s_ref[nslot, hh, :, cols] = s_n
                    mx[si] = jnp.max(s_n, axis=0, keepdims=True)
                if q0 >= c0:
                    s = s_ref[slot, hh, :, cols]
                    if consume_masked and q0 == c0:
                        s = jnp.where(tri, s, NEG_BIG)
                        s_max = jnp.max(s, axis=0, keepdims=True)
                    else:
                        s_max = maxima[hh][si]
                    m_new = jnp.maximum(m[:, cols], s_max)
                    p = jnp.exp2(s - m_new).astype(BF16)
                    acc_ref[hh, :, cols] = (jnp.exp2(m[:, cols] - m_new) * acc_ref[hh, :, cols]
                                            + _dot(vt_ref[j, hh], p))
                    m_parts.append(m_new)
            new_ms.append(jnp.concatenate(m_parts, axis=1))
            new_maxima.append(tuple(mx))
        return tuple(new_ms), tuple(new_maxima)

    def first_scores():
        maxima = []
        for hh in range(2):
            s = lax.dot_general(k_ref[hh, 0:tk, :], qs[hh], NT_DIMS, preferred_element_type=F32)
            s_ref[0, hh] = s
            mx = jnp.max(s, axis=0, keepdims=True)
            maxima.append(tuple(mx[:, si * QSTRIP:(si + 1) * QSTRIP] for si in range(n_strips)))
        return tuple(maxima)

    n_sub = tq // tk

    def body(jj, carry):
        ms, maxima = carry
        for b in range(n_sub):
            j = n_sub * jj + b
            ms, maxima = step((b % 2, j, 0, False), ((b + 1) % 2, j + 1, 0, False), ms, maxima)
        return ms, maxima

    acc_ref[...] = jnp.zeros(acc_ref.shape, F32)
    m0 = jnp.full((1, tq), NEG_BIG, F32)
    ms, maxima = lax.fori_loop(0, qi, body, ((m0, m0), first_scores()))
    for b in range(n_sub):
        j = n_sub * qi + b
        nxt = ((b + 1) % 2, j + 1, (b + 1) * tk, True) if b + 1 < n_sub else None
        ms, maxima = step((b % 2, j, b * tk, b == 0), nxt, ms, maxima)
    outs = [acc_ref[hh, :dh, :] / acc_ref[hh, dh:dh + 1, :] for hh in range(2)]
    o_ref[...] = jnp.concatenate(outs, axis=0).T.astype(o_ref.dtype)


def _flash(q_aug, k_aug, vt):
    n_heads, s, _ = q_aug.shape
    n_kv, _, vt_rows, tk = vt.shape
    d = n_heads * (LANES // 2)
    assert TQ % (2 * tk) == 0 and s % TQ == 0 and tk == QSTRIP
    return pl.pallas_call(
        _flash_kernel,
        grid=(n_heads // 2, s // TQ),
        in_specs=[
            pl.BlockSpec((2, TQ, LANES), lambda hp, qi: (hp, qi, 0)),
            pl.BlockSpec((2, s, LANES), lambda hp, qi: (hp, 0, 0)),
            pl.BlockSpec((n_kv, 2, vt_rows, tk), lambda hp, qi: (0, hp, 0, 0)),
        ],
        out_specs=pl.BlockSpec((TQ, LANES), lambda hp, qi: (qi, hp)),
        out_shape=jax.ShapeDtypeStruct((s, d), BF16),
        scratch_shapes=[pltpu.VMEM((2, 2, tk, TQ), F32), pltpu.VMEM((2, vt_rows, TQ), F32)],
        compiler_params=pltpu.CompilerParams(
            dimension_semantics=("arbitrary", "arbitrary"), vmem_limit_bytes=VMEM_LIMIT_BYTES),
        name="flash_attn",
    )(q_aug, k_aug, vt)


def _rglru_kernel(x_ref, wx_ref, wg_ref, cw_ref, cb_ref, wa_ref, ba_ref, wi_ref, bi_ref,
                  lam_ref, wo_ref, g_ref, b_ref, y_ref, xprev_ref, hprev_ref):
    tm, d = x_ref.shape
    n_blocks = wa_ref.shape[0]
    blk = d // n_blocks
    n_taps = cw_ref.shape[0]

    @pl.when(pl.program_id(0) == 0)
    def _():
        xprev_ref[...] = jnp.zeros_like(xprev_ref)
        hprev_ref[...] = jnp.zeros_like(hprev_ref)

    x = x_ref[...]
    xb = x.astype(BF16)
    xp = _dot(xb, wx_ref[...])
    gb = _dot(xb, wg_ref[...])

    prev8 = xprev_ref[...]
    xc = cb_ref[...] + cw_ref[n_taps - 1:n_taps, :] * xp
    for k in range(1, n_taps):
        xc = xc + cw_ref[n_taps - 1 - k:n_taps - k, :] * _shift_rows(prev8, xp, k)
    xprev_ref[...] = xp[tm - SUBLANES:, :]

    r_parts, i_parts = [], []
    for n in range(n_blocks):
        xh = xc[:, n * blk:(n + 1) * blk].astype(BF16)
        r_parts.append(_dot(xh, wa_ref[n]))
        i_parts.append(_dot(xh, wi_ref[n]))
    r = _sigmoid(jnp.concatenate(r_parts, axis=1) + ba_ref[...])
    ig = _sigmoid(jnp.concatenate(i_parts, axis=1) + bi_ref[...])

    log_a = (-LRU_C) * r * jax.nn.softplus(-lam_ref[...])
    a = jnp.exp(log_a)
    z = -jnp.tanh(log_a) * (a * a + 1.0)
    u = jnp.where(z > 0.0, z * lax.rsqrt(z), 0.0) * (ig * xc)

    n_groups = tm // SUBLANES
    a = a.reshape(n_groups, SUBLANES, d)
    u = u.reshape(n_groups, SUBLANES, d)
    sub = lax.broadcasted_iota(jnp.int32, (1, SUBLANES, d), 1)
    dist = 1
    while dist < SUBLANES:
        a_sh = jnp.where(sub >= dist, pltpu.roll(a, dist, 1), 1.0)
        u_sh = jnp.where(sub >= dist, pltpu.roll(u, dist, 1), 0.0)
        u = u + a * u_sh
        a = a * a_sh
        dist *= 2
    h_last = hprev_ref[0:1, :]
    groups = []
    for g in range(n_groups):
        h_g = u[g] + a[g] * h_last
        groups.append(h_g)
        h_last = h_g[SUBLANES - 1:SUBLANES, :]
    h = jnp.concatenate(groups, axis=0)
    hprev_ref[...] = jnp.broadcast_to(h_last, hprev_ref.shape)

    yv = (h * jax.nn.gelu(gb)).astype(BF16)
    m = _dot(yv, wo_ref[...])
    y_ref[...] = _layer_norm(ALPHA * x + m, g_ref[...], b_ref[...])


def _rglru_layer(x, w_in, conv_w, conv_b, w_a, b_a, w_i, b_i, lam, w_out, j, g, b, i):
    s, d = x.shape
    _, n_blocks, blk, _ = w_a.shape
    n_taps = conv_w.shape[1]
    assert s % TM_LRU == 0 and n_taps - 1 < SUBLANES
    vec = _layer_block((1, d), j, 0, 0)
    return pl.pallas_call(
        _rglru_kernel,
        grid=(s // TM_LRU,),
        in_specs=[
            pl.BlockSpec((TM_LRU, d), lambda r: (r, 0)),
            _layer_block((d, d), j, 0, 0), _layer_block((d, d), j, 0, 1),
            _layer_block((n_taps, d), j, 0, 0), vec,
            _layer_block((n_blocks, blk, blk), j, 0, 0, 0), vec,
            _layer_block((n_blocks, blk, blk), j, 0, 0, 0), vec,
            vec,
            _layer_block((d, d), j, 0, 0),
            _layer_block((1, d), i, 0, 0), _layer_block((1, d), i, 0, 0),
        ],
        out_specs=pl.BlockSpec((TM_LRU, d), lambda i: (i, 0)),
        out_shape=jax.ShapeDtypeStruct((s, d), F32),
        scratch_shapes=[pltpu.VMEM((SUBLANES, d), F32), pltpu.VMEM((SUBLANES, d), F32)],
        compiler_params=pltpu.CompilerParams(
            dimension_semantics=("arbitrary",), vmem_limit_bytes=VMEM_LIMIT_BYTES),
        name="rglru",
    )(x, w_in, w_in, conv_w, conv_b, w_a, b_a, w_i, b_i, lam, w_out, g, b)


def _ffn_kernel(*refs, attn_out):
    if attn_out:
        o_ref, wo_ref, g1_ref, b1_ref, *refs = refs
    (x_ref, wv_ref, wg_ref, cw_ref, cb_ref, wd_ref, g_ref, b_ref,
     p_ref, wp_ref, wpg_ref, bpg_ref, y_ref, prev_ref, h_ref, act_ref) = refs
    tm, d = x_ref.shape
    d_ff = wd_ref.shape[0]
    tn = h_ref.shape[3]
    n_chunks = d_ff // tn
    n_taps = cw_ref.shape[0]

    @pl.when(pl.program_id(0) == 0)
    def _():
        prev_ref[...] = jnp.zeros_like(prev_ref)

    strips = [slice(r0, r0 + FFN_TAIL_ROWS) for r0 in range(0, tm, FFN_TAIL_ROWS)]
    x = x_ref[...]
    if attn_out:
        mixed = [_dot(o_ref[rows, :], wo_ref[...]) for rows in strips]
        x = jnp.concatenate([_layer_norm(ALPHA * x[rows] + m, g1_ref[...], b1_ref[...])
                             for rows, m in zip(strips, mixed)], axis=0)
    xb = x.astype(BF16)

    def up(slot, j):
        cols = slice(j * tn, (j + 1) * tn)
        h_ref[slot, 0] = _dot(xb, wv_ref[:, cols])
        h_ref[slot, 1] = _dot(xb, wg_ref[:, cols])

    def conv(slot, part, j):
        cols = slice(part * d_ff + j * tn, part * d_ff + (j + 1) * tn)
        h = h_ref[slot, part]
        prev8 = prev_ref[:, cols]
        out = cb_ref[:, cols] + cw_ref[n_taps - 1:n_taps, cols] * h
        for k in range(1, n_taps):
            out = out + cw_ref[n_taps - 1 - k:n_taps - k, cols] * _shift_rows(prev8, h, k)
        prev_ref[:, cols] = h[tm - SUBLANES:, :]
        return out

    up(0, 0)
    for j in range(n_chunks):
        if j + 1 < n_chunks:
            up((j + 1) % 2, j + 1)
        val = conv(j % 2, 0, j)
        gate = conv(j % 2, 1, j)
        act_ref[:, j * tn:(j + 1) * tn] = (jax.nn.gelu(gate) * val).astype(BF16)

    ffs = [_dot(act_ref[strips[0], :], wd_ref[...])]
    for n, rows in enumerate(strips):
        if n + 1 < len(strips):
            ffs.append(_dot(act_ref[strips[n + 1], :], wd_ref[...]))
        x2 = _layer_norm(ALPHA * x[rows] + ffs[n], g_ref[...], b_ref[...])
        gate = _sigmoid(_dot(x2.astype(BF16), wpg_ref[...]) + bpg_ref[...])
        y_ref[rows, :] = x2 + gate * _dot(p_ref[rows, :].astype(BF16), wp_ref[...])


def _ffn_layer(x, w_up, conv_w, conv_b, w_down, g, b, p, ple_w, ple_gate_w, ple_gate_b, i, bi,
               attn=None):
    s, d = x.shape
    d_ff = w_down.shape[1]
    d_ple = p.shape[3]
    n_taps = conv_w.shape[1]
    assert s % TM_FFN == 0 and d_ff % TN_FFN == 0 and n_taps - 1 < SUBLANES
    vec = _layer_block((1, d), i, 0, 0)
    attn_specs, attn_args = [], ()
    if attn is not None:
        o, w_out, j, g1, b1 = attn
        attn_specs = [pl.BlockSpec((TM_FFN, d), lambda r: (r, 0)), _layer_block((d, d), j, 0, 0),
                      vec, vec]
        attn_args = (o, w_out, g1, b1)
    return pl.pallas_call(
        functools.partial(_ffn_kernel, attn_out=attn is not None),
        grid=(s // TM_FFN,),
        in_specs=attn_specs + [
            pl.BlockSpec((TM_FFN, d), lambda r: (r, 0)),
            _layer_block((d, d_ff), i, 0, 0), _layer_block((d, d_ff), i, 0, 1),
            _layer_block((n_taps, 2 * d_ff), i, 0, 0), _layer_block((1, 2 * d_ff), i, 0, 0),
            _layer_block((d_ff, d), i, 0, 0), vec, vec,
            pl.BlockSpec((None, None, TM_FFN, d_ple), lambda r: (i, bi, r, 0)),
            _layer_block((d_ple, d), i, 0, 0), _layer_block((d, d), i, 0, 0), vec,
        ],
        out_specs=pl.BlockSpec((TM_FFN, d), lambda i: (i, 0)),
        out_shape=jax.ShapeDtypeStruct((s, d), F32),
        scratch_shapes=[pltpu.VMEM((SUBLANES, 2 * d_ff), F32),
                        pltpu.VMEM((2, 2, TM_FFN, TN_FFN), F32),
                        pltpu.VMEM((TM_FFN, d_ff), BF16)],
        compiler_params=pltpu.CompilerParams(
            dimension_semantics=("arbitrary",), vmem_limit_bytes=VMEM_LIMIT_BYTES),
        name="conv_ffn",
    )(*attn_args, x, w_up, w_up, conv_w, conv_b, w_down, g, b, p, ple_w, ple_gate_w, ple_gate_b)


def kernel(x, p, a_w_in, a_b_f, a_w_out, b_w_in, b_conv_w, b_conv_b, b_w_a, b_b_a, b_w_i, b_b_i,
           b_lam, b_w_out, f_w_up, f_conv_w, f_conv_b, f_w_down, ln1_g, ln1_b, ln2_g, ln2_b,
           ple_w, ple_gate_w, ple_gate_b):
    bsz, s, d = x.shape
    depth = p.shape[0]
    n_heads = a_b_f.shape[1]
    bf16 = lambda w: w.astype(BF16)
    attn_w = _attn_weights(a_w_in, a_b_f)
    a_w_out, b_w_in, b_w_a, b_w_i, b_w_out = map(bf16, (a_w_out, b_w_in, b_w_a, b_w_i, b_w_out))
    f_w_up, f_w_down, ple_w, ple_gate_w = map(bf16, (f_w_up, f_w_down, ple_w, ple_gate_w))
    b_b_a, b_b_i = (v.reshape(v.shape[0], 1, d) for v in (b_b_a, b_b_i))
    b_conv_b, b_lam, f_conv_b, ple_gate_b = map(_rows, (b_conv_b, b_lam, f_conv_b, ple_gate_b))
    ln1_g, ln1_b, ln2_g, ln2_b = map(_rows, (ln1_g, ln1_b, ln2_g, ln2_b))
    outs = []
    for bi in range(bsz):
        xs = x[bi]
        for i in range(depth):
            j = i // 2
            attn = None
            if i % 2 == 0:
                q_aug, k_aug, vt = _attn_proj(xs, attn_w, j, n_heads)
                attn = (_flash(q_aug, k_aug, vt), a_w_out, j, ln1_g, ln1_b)
            else:
                xs = _rglru_layer(xs, b_w_in, b_conv_w, b_conv_b, b_w_a, b_b_a, b_w_i, b_b_i,
                                  b_lam, b_w_out, j, ln1_g, ln1_b, i)
            xs = _ffn_layer(xs, f_w_up, f_conv_w, f_conv_b, f_w_down, ln2_g, ln2_b, p,
                            ple_w, ple_gate_w, ple_gate_b, i, bi, attn)
        outs.append(xs)
    return jnp.stack(outs, axis=0)
```
